```python
import jax, jax.numpy as jnp
from jax import lax
import numpy as np


D_MODEL = 1024
BATCH = 2
SEQ = 8192
DEPTH = 2

POOL_WINDOWS = (2, 4, 8, 16)
POOL_GROUPS = len(POOL_WINDOWS)
POOL_WIDTH = D_MODEL
POOL_GROUP = POOL_WIDTH // POOL_GROUPS
SGU_CHUNK = 128
SGU_WIDTH = D_MODEL
SGU_HEADS = 8
SGU_HEAD_DIM = SGU_WIDTH // SGU_HEADS
N_BRANCHES = 2
IN_WIDTH = POOL_WIDTH + 2 * SGU_WIDTH + N_BRANCHES * D_MODEL
D_FF = 2816
CONV_WIDTH = 3
PLE_DIM = 256
EPS = 1e-6

kernel_name = "hybrid_pool_sgu_convffn_ple"


def rmsnorm(x, g):
    xf = x.astype(jnp.float32)
    y = xf * lax.rsqrt(jnp.mean(xf * xf, axis=-1, keepdims=True) + EPS)
    return (y * g.astype(jnp.float32)).astype(x.dtype)


def pool_mixer(h, w_pool, pool_scale):
    T = h.shape[1]
    hf = h.astype(jnp.float32)
    c = jnp.cumsum(hf, axis=1)
    t = jnp.arange(T)
    outs = []
    for gi, w in enumerate(POOL_WINDOWS):
        sl = slice(gi * POOL_GROUP, (gi + 1) * POOL_GROUP)
        cg = c[..., sl]
        prev = jnp.pad(cg, ((0, 0), (w, 0), (0, 0)))[:, :T]
        cnt = jnp.minimum(t + 1, w).astype(jnp.float32)[None, :, None]
        outs.append((cg - prev) / cnt - hf[..., sl])
    pooled = jnp.stack(outs, axis=2).astype(h.dtype)
    y = jnp.einsum('btgc,gcd->btgd', pooled, w_pool)
    return y.reshape(h.shape) * pool_scale


def spatial_gating(z_uv, sgu_norm, w_spatial, b_spatial):
    B, T, _ = z_uv.shape
    z = jax.nn.gelu(z_uv, approximate=False)
    u, v = z[..., :SGU_WIDTH], z[..., SGU_WIDTH:]
    v = rmsnorm(v, sgu_norm)
    nc = T // SGU_CHUNK
    v = v.reshape(B, nc, SGU_CHUNK, SGU_HEADS, SGU_HEAD_DIM)
    mask = jnp.tril(jnp.ones((SGU_CHUNK, SGU_CHUNK), dtype=w_spatial.dtype))
    ws = w_spatial * mask[None]
    mixed = jnp.einsum('hts,bnshd->bnthd', ws, v)
    mixed = mixed + jnp.transpose(b_spatial)[None, None, :, :, None]
    return u * mixed.reshape(B, T, SGU_WIDTH)


def conv_ffn(h, w_up, conv_w, conv_b, w_down):
    T = h.shape[1]
    up = h @ w_up
    up_pad = jnp.pad(up, ((0, 0), (CONV_WIDTH - 1, 0), (0, 0)))
    conv = conv_b
    for k in range(CONV_WIDTH):
        conv = conv + conv_w[k] * up_pad[:, k:k + T]
    a, b = conv[..., :D_FF], conv[..., D_FF:]
    return (jax.nn.gelu(a, approximate=False) * b) @ w_down


def setup_inputs(seed: int = 0) -> dict:
    key = jax.random.key(seed)
    ks = jax.random.split(key, 24)
    f32 = jnp.float32
    L, D = DEPTH, D_MODEL

    def nrm(k, shape, scale):
        return jax.random.normal(k, shape, f32) * scale

    def gain(k, shape):
        return 1.0 + 0.05 * jax.random.normal(k, shape, f32)

    return {
        "x": nrm(ks[0], (BATCH, SEQ, D), 1.0),
        "p": nrm(ks[1], (DEPTH, BATCH, SEQ, PLE_DIM), 1.0),
        "mix_norm": gain(ks[2], (L, D)),
        "w_in": nrm(ks[3], (L, D, IN_WIDTH), D ** -0.5),
        "w_pool": nrm(ks[4], (L, POOL_GROUPS, POOL_GROUP, POOL_GROUP), POOL_GROUP ** -0.5),
        "pool_scale": gain(ks[5], (L, POOL_WIDTH)),
        "sgu_norm": gain(ks[6], (L, SGU_WIDTH)),
        "w_spatial": nrm(ks[7], (L, SGU_HEADS, SGU_CHUNK, SGU_CHUNK), 0.5 * SGU_CHUNK ** -0.5),
        "b_spatial": gain(ks[8], (L, SGU_HEADS, SGU_CHUNK)),
        "w_branch_a": nrm(ks[9], (L, POOL_WIDTH, D), POOL_WIDTH ** -0.5),
        "w_branch_b": nrm(ks[10], (L, SGU_WIDTH, D), SGU_WIDTH ** -0.5),
        "w_out": nrm(ks[11], (L, D, D), D ** -0.5),
        "ffn_norm": gain(ks[12], (L, D)),
        "w_up": nrm(ks[13], (L, D, 2 * D_FF), D ** -0.5),
        "conv_w": nrm(ks[14], (L, CONV_WIDTH, 2 * D_FF), CONV_WIDTH ** -0.5),
        "conv_b": nrm(ks[15], (L, 2 * D_FF), 0.02),
        "w_down": nrm(ks[16], (L, D_FF, D), D_FF ** -0.5),
        "ple_norm": gain(ks[17], (L, D)),
        "w_ple_gate": nrm(ks[18], (L, D, D), D ** -0.5),
        "w_ple": nrm(ks[19], (L, PLE_DIM, D), PLE_DIM ** -0.5),
        "final_norm": gain(ks[20], (D,)),
    }


def reference(x, p, mix_norm, w_in, w_pool, pool_scale, sgu_norm, w_spatial, b_spatial,
              w_branch_a, w_branch_b, w_out, ffn_norm, w_up, conv_w, conv_b, w_down,
              ple_norm, w_ple_gate, w_ple, final_norm):
    o_uv = POOL_WIDTH
    o_gate = POOL_WIDTH + 2 * SGU_WIDTH
    for i in range(DEPTH):
        h = rmsnorm(x, mix_norm[i])
        z = h @ w_in[i]
        z_pool = z[..., :o_uv]
        z_uv = z[..., o_uv:o_gate]
        z_gate = z[..., o_gate:]
        y_a = pool_mixer(z_pool, w_pool[i], pool_scale[i]) @ w_branch_a[i]
        y_b = spatial_gating(z_uv, sgu_norm[i], w_spatial[i], b_spatial[i]) @ w_branch_b[i]
        gates = jax.nn.sigmoid(z_gate.astype(jnp.float32)).astype(x.dtype)
        g_a, g_b = gates[..., :D_MODEL], gates[..., D_MODEL:]
        x = x + (g_a * y_a + g_b * y_b) @ w_out[i]
        h = rmsnorm(x, ffn_norm[i])
        x = x + conv_ffn(h, w_up[i], conv_w[i], conv_b[i], w_down[i])
        gate = jax.nn.sigmoid((rmsnorm(x, ple_norm[i]) @ w_ple_gate[i]).astype(jnp.float32)).astype(x.dtype)
        x = x + gate * (p[i] @ w_ple[i])
    return rmsnorm(x, final_norm)
```

```python
import functools

import jax
import jax.numpy as jnp
import numpy as np
from jax import lax
from jax.experimental import pallas as pl
from jax.experimental.pallas import tpu as pltpu

D_MODEL = 1024
POOL_WINDOWS = (2, 4, 8, 16)
POOL_GROUP = D_MODEL // len(POOL_WINDOWS)
POOL_HALO = 16
SGU_CHUNK = 128
SGU_HEADS = 8
SGU_HEAD_DIM = D_MODEL // SGU_HEADS
D_FF = 2816
FF_CHUNK = 256
CONV_WIDTH = 3
CONV_HALO = 8
PLE_DIM = 256
EPS = 1e-6

TOKEN_TILE = 256
V7X_VMEM_LIMIT_BYTES = 56 * 1024 * 1024

_SQRT_HALF = float(np.sqrt(0.5))


def _rmsnorm(x, g):
    return x * lax.rsqrt(jnp.mean(x * x, axis=-1, keepdims=True) + EPS) * g


def _gelu(x):
    return 0.5 * x * (1.0 + lax.erf(x * _SQRT_HALF))


def _sigmoid(x):
    return 1.0 / (1.0 + jnp.exp(-x))


def _dot(a, b):
    return jnp.dot(a, b, preferred_element_type=jnp.float32)


def _bf16(x):
    return x.astype(jnp.bfloat16)


def _mixer_kernel(seq_len, x_ref, g_ref, w_in_ref, w_pool_ref, pscale_ref, sgu_g_ref,
                  ws_ref, bs_ref, wa_ref, wb_ref, wo_ref, o_ref, zp_ref):
    tm = x_ref.shape[0]
    t0 = (pl.program_id(0) * tm) & (seq_len - 1)

    @pl.when(t0 == 0)
    def _():
        zp_ref[0:POOL_HALO, :] = jnp.zeros((POOL_HALO, D_MODEL), jnp.float32)

    x = x_ref[...]
    h = _bf16(_rmsnorm(x, g_ref[...]))

    zp_ref[POOL_HALO:POOL_HALO + tm, :] = _dot(h, w_in_ref[:, 0:D_MODEL])
    t_pos = lax.broadcasted_iota(jnp.int32, (tm, POOL_GROUP), 0) + t0
    y_groups = []
    for gi, w in enumerate(POOL_WINDOWS):
        cols = slice(gi * POOL_GROUP, (gi + 1) * POOL_GROUP)
        z_self = zp_ref[POOL_HALO:POOL_HALO + tm, cols]
        s = z_self
        for k in range(1, w):
            s = s + zp_ref[POOL_HALO - k:POOL_HALO - k + tm, cols]
        cnt = jnp.minimum(t_pos + 1, w).astype(jnp.float32)
        pooled = s / cnt - z_self
        y_groups.append(_dot(_bf16(pooled), w_pool_ref[gi]))
    y_pool = jnp.concatenate(y_groups, axis=1) * pscale_ref[...]
    y_a = _dot(_bf16(y_pool), wa_ref[...])
    zp_ref[0:POOL_HALO, :] = zp_ref[tm:tm + POOL_HALO, :]

    u = _gelu(_dot(h, w_in_ref[:, D_MODEL:2 * D_MODEL]))
    v = _gelu(_dot(h, w_in_ref[:, 2 * D_MODEL:3 * D_MODEL]))
    vn = _bf16(_rmsnorm(v, sgu_g_ref[...]))
    row = lax.broadcasted_iota(jnp.int32, (SGU_CHUNK, SGU_CHUNK), 0)
    col = lax.broadcasted_iota(jnp.int32, (SGU_CHUNK, SGU_CHUNK), 1)
    tril = row >= col
    ws = [_bf16(jnp.where(tril, ws_ref[hh], 0.0)) for hh in range(SGU_HEADS)]
    mixed_rows = []
    for c in range(tm // SGU_CHUNK):
        rows = slice(c * SGU_CHUNK, (c + 1) * SGU_CHUNK)
        heads = []
        for hh in range(SGU_HEADS):
            hcols = slice(hh * SGU_HEAD_DIM, (hh + 1) * SGU_HEAD_DIM)
            heads.append(_dot(ws[hh], vn[rows, hcols]))
        mixed_rows.append(jnp.concatenate(heads, axis=1) + bs_ref[...])
    mixed = jnp.concatenate(mixed_rows, axis=0)
    y_b = _dot(_bf16(u * mixed), wb_ref[...])

    g_a = _sigmoid(_dot(h, w_in_ref[:, 3 * D_MODEL:4 * D_MODEL]))
    g_b = _sigmoid(_dot(h, w_in_ref[:, 4 * D_MODEL:5 * D_MODEL]))
    o_ref[...] = x + _dot(_bf16(g_a * y_a + g_b * y_b), wo_ref[...])


def _ffn_kernel(seq_len, final, x_ref, p_ref, g_ref, w_up_ref, cw_ref, cb_ref, w_down_ref,
                pg_ref, w_gate_ref, w_ple_ref, fg_ref, o_ref, up_ref, act_ref):
    tm = x_ref.shape[0]
    t0 = (pl.program_id(0) * tm) & (seq_len - 1)

    @pl.when(t0 == 0)
    def _():
        up_ref[0:CONV_HALO, :] = jnp.zeros((CONV_HALO, 2 * D_FF), jnp.float32)

    x = x_ref[...]
    h = _bf16(_rmsnorm(x, g_ref[...]))

    def conv(cols):
        up_ref[CONV_HALO:CONV_HALO + tm, cols] = _dot(h, w_up_ref[:, cols])
        out = cb_ref[:, cols]
        for k in range(CONV_WIDTH):
            lo = CONV_HALO - (CONV_WIDTH - 1) + k
            out = out + cw_ref[k:k + 1, cols] * up_ref[lo:lo + tm, cols]
        return out

    for j in range(D_FF // FF_CHUNK):
        a = conv(slice(j * FF_CHUNK, (j + 1) * FF_CHUNK))
        b = conv(slice(D_FF + j * FF_CHUNK, D_FF + (j + 1) * FF_CHUNK))
        act_ref[:, j * FF_CHUNK:(j + 1) * FF_CHUNK] = _bf16(_gelu(a) * b)
    up_ref[0:CONV_HALO, :] = up_ref[tm:tm + CONV_HALO, :]
    x = x + _dot(act_ref[...], w_down_ref[...])

    gate = _sigmoid(_dot(_bf16(_rmsnorm(x, pg_ref[...])), w_gate_ref[...]))
    x = x + gate * _dot(_bf16(p_ref[...]), w_ple_ref[...])
    if final:
        x = _rmsnorm(x, fg_ref[...])
    o_ref[...] = x


def _resident(shape):
    return pl.BlockSpec(shape, lambda i: (0,) * len(shape), pipeline_mode=pl.Buffered(1))


def _tiled(tm, width):
    return pl.BlockSpec((tm, width), lambda i: (i, 0))


def _compiler_params():
    return pltpu.CompilerParams(dimension_semantics=("arbitrary",),
                                vmem_limit_bytes=V7X_VMEM_LIMIT_BYTES)


def _mixer(x, seq_len, g, w_in, w_pool, pscale, sgu_g, ws, bs, wa, wb, wo):
    n, d = x.shape
    tm = TOKEN_TILE
    weights = (g, w_in, w_pool, pscale, sgu_g, ws, bs, wa, wb, wo)
    return pl.pallas_call(
        functools.partial(_mixer_kernel, seq_len),
        grid=(n // tm,),
        in_specs=[_tiled(tm, d)] + [_resident(w.shape) for w in weights],
        out_specs=_tiled(tm, d),
        out_shape=jax.ShapeDtypeStruct((n, d), jnp.float32),
        scratch_shapes=[pltpu.VMEM((tm + POOL_HALO, d), jnp.float32)],
        compiler_params=_compiler_params(),
        name="mixer",
    )(x, *weights)


def _ffn(x, p, seq_len, final, g, w_up, cw, cb, w_down, pg, w_gate, w_ple, fg):
    n, d = x.shape
    tm = TOKEN_TILE
    weights = (g, w_up, cw, cb, w_down, pg, w_gate, w_ple, fg)
    return pl.pallas_call(
        functools.partial(_ffn_kernel, seq_len, final),
        grid=(n // tm,),
        in_specs=[_tiled(tm, d), _tiled(tm, PLE_DIM)] + [_resident(w.shape) for w in weights],
        out_specs=_tiled(tm, d),
        out_shape=jax.ShapeDtypeStruct((n, d), jnp.float32),
        scratch_shapes=[pltpu.VMEM((tm + CONV_HALO, 2 * D_FF), jnp.float32),
                        pltpu.VMEM((tm, D_FF), jnp.bfloat16)],
        compiler_params=_compiler_params(),
        name="ffn_final" if final else "ffn",
    )(x, p, *weights)


def kernel(x, p, mix_norm, w_in, w_pool, pool_scale, sgu_norm, w_spatial, b_spatial, w_branch_a, w_branch_b, w_out, ffn_norm, w_up, conv_w, conv_b, w_down, ple_norm, w_ple_gate, w_ple, final_norm):
    batch, seq_len, d = x.shape
    depth = w_in.shape[0]
    assert d == D_MODEL and seq_len % TOKEN_TILE == 0 and seq_len & (seq_len - 1) == 0
    assert TOKEN_TILE % SGU_CHUNK == 0
    n = batch * seq_len
    row = lambda v: v.reshape(1, -1)
    xf = x.reshape(n, d)
    for i in range(depth):
        bs = jnp.repeat(b_spatial[i].T, SGU_HEAD_DIM, axis=1)
        xf = _mixer(xf, seq_len, row(mix_norm[i]), _bf16(w_in[i]), _bf16(w_pool[i]),
                    row(pool_scale[i]), row(sgu_norm[i]), w_spatial[i], bs,
                    _bf16(w_branch_a[i]), _bf16(w_branch_b[i]), _bf16(w_out[i]))
        xf = _ffn(xf, p[i].reshape(n, PLE_DIM), seq_len, i == depth - 1, row(ffn_norm[i]),
                  _bf16(w_up[i]), conv_w[i], row(conv_b[i]), _bf16(w_down[i]),
                  row(ple_norm[i]), _bf16(w_ple_gate[i]), _bf16(w_ple[i]), row(final_norm))
    return xf.reshape(batch, seq_len, d)
```

```python
import functools

import jax
import jax.numpy as jnp
import numpy as np
from jax import lax
from jax.experimental import pallas as pl
from jax.experimental.pallas import tpu as pltpu

LANES = 128
SLAB = 256
D_MODEL = 1024
N_SLABS = D_MODEL // SLAB
POOL_WINDOWS = (2, 4, 8, 16)
POOL_HALO = 16
SGU_CHUNK = 128
SGU_HEADS = 8
SGU_HEAD_DIM = D_MODEL // SGU_HEADS
D_FF = 2816
FF_SLABS = D_FF // SLAB
CONV_WIDTH = 3
CONV_HALO = 8
PLE_DIM = 256
EPS = 1e-6

TOKEN_TILE = 256
V7X_VMEM_LIMIT_BYTES = 56 * 1024 * 1024

_SQRT_HALF = float(np.sqrt(0.5))


def _gelu(x):
    return (0.5 * x) * (1.0 + lax.erf(x * _SQRT_HALF))


def _sigmoid(x):
    return 1.0 / (1.0 + jnp.exp(-x))


def _dot(a, b):
    return jnp.dot(a, b, preferred_element_type=jnp.float32)


def _bf16(x):
    return x.astype(jnp.bfloat16)


def _cat(slabs):
    return jnp.concatenate(slabs, axis=1)


def _rms_scale(slabs):
    ssq = sum(jnp.sum(s * s, axis=-1, keepdims=True) for s in slabs)
    width = sum(s.shape[1] for s in slabs)
    return lax.rsqrt(ssq * (1.0 / width) + EPS)


def _rmsnorm_bf16(x, g_ref):
    return _bf16(x * _rms_scale([x]) * g_ref[...])


def _col(ref, j):
    return ref[:, j * SLAB:(j + 1) * SLAB]


def _mixer_kernel(seq_len, x_ref, g_ref, w_in_ref, w_pool_ref, pscale_ref, sgu_g_ref,
                  ws_ref, bs_ref, wa_ref, wb_ref, wo_ref, o_ref, zp_ref):
    tm = x_ref.shape[0]
    t0 = (pl.program_id(0) * tm) & (seq_len - 1)

    @pl.when(t0 == 0)
    def _():
        zp_ref[:, 0:POOL_HALO, :] = jnp.zeros((2 * N_SLABS, POOL_HALO, LANES), jnp.float32)

    x = x_ref[...]
    h = _rmsnorm_bf16(x, g_ref)

    slab_dots = lambda first: [_dot(h, w_in_ref[first + j]) for j in range(N_SLABS)]

    z = slab_dots(0)
    u = slab_dots(N_SLABS)
    v = slab_dots(2 * N_SLABS)
    t_pos = lax.broadcasted_iota(jnp.int32, (tm, LANES), 0) + t0
    y_pool = []
    for gi, w in enumerate(POOL_WINDOWS):
        inv_cnt = 1.0 / jnp.minimum(t_pos + 1, w).astype(jnp.float32)
        pooled = []
        for half in range(SLAB // LANES):
            s = 2 * gi + half
            z_self = z[gi][:, half * LANES:(half + 1) * LANES]
            zp_ref[s, POOL_HALO:POOL_HALO + tm, :] = z_self
            acc = z_self
            for k in range(1, w):
                acc = acc + zp_ref[s, POOL_HALO - k:POOL_HALO - k + tm, :]
            pooled.append(acc * inv_cnt - z_self)
        y_pool.append(_dot(_bf16(_cat(pooled)), w_pool_ref[gi]) * _col(pscale_ref, gi))
    zp_ref[:, 0:POOL_HALO, :] = zp_ref[:, tm:tm + POOL_HALO, :]
    y_pool = _bf16(_cat(y_pool))
    g_a = slab_dots(3 * N_SLABS)
    y_a = [_dot(y_pool, wa_ref[j]) for j in range(N_SLABS)]

    u = [_gelu(s) for s in u]
    v = [_gelu(s) for s in v]
    v_scale = _rms_scale(v)
    vn = [_bf16(v[j] * v_scale * _col(sgu_g_ref, j)) for j in range(N_SLABS)]
    row = lax.broadcasted_iota(jnp.int32, (SGU_CHUNK, SGU_CHUNK), 0)
    col = lax.broadcasted_iota(jnp.int32, (SGU_CHUNK, SGU_CHUNK), 1)
    tril = row >= col
    ws = [_bf16(jnp.where(tril, ws_ref[hh], 0.0)) for hh in range(SGU_HEADS)]
    mixed = []
    for j in range(N_SLABS):
        chunks = []
        for c in range(tm // SGU_CHUNK):
            rows = slice(c * SGU_CHUNK, (c + 1) * SGU_CHUNK)
            heads = []
            for half in range(SLAB // SGU_HEAD_DIM):
                hcols = slice(half * SGU_HEAD_DIM, (half + 1) * SGU_HEAD_DIM)
                heads.append(_dot(ws[2 * j + half], vn[j][rows, hcols]))
            chunks.append(_cat(heads) + _col(bs_ref, j))
        mixed.append(jnp.concatenate(chunks, axis=0))
    g_b = slab_dots(4 * N_SLABS)
    sgu = _bf16(_cat([u[j] * mixed[j] for j in range(N_SLABS)]))
    y_b = [_dot(sgu, wb_ref[j]) for j in range(N_SLABS)]

    merged = _bf16(_cat([_sigmoid(g_a[j]) * y_a[j] + _sigmoid(g_b[j]) * y_b[j]
                         for j in range(N_SLABS)]))
    for j in range(N_SLABS):
        o_ref[:, j * SLAB:(j + 1) * SLAB] = _col(x_ref, j) + _dot(merged, wo_ref[j])


def _ffn_kernel(seq_len, final, x_ref, p_ref, g_ref, w_up_ref, cw_ref, cb_ref, w_down_ref,
                pg_ref, w_gate_ref, w_ple_ref, fg_ref, o_ref, up_ref, act_ref):
    tm = x_ref.shape[0]
    t0 = (pl.program_id(0) * tm) & (seq_len - 1)

    @pl.when(t0 == 0)
    def _():
        up_ref[:, 0:CONV_HALO, :] = jnp.zeros((up_ref.shape[0], CONV_HALO, LANES), jnp.float32)

    x = x_ref[...]
    h = _rmsnorm_bf16(x, g_ref)

    def conv(slab):
        up = _dot(h, w_up_ref[slab])
        halves = []
        for half in range(SLAB // LANES):
            s = 2 * slab + half
            cols = slice(s * LANES, (s + 1) * LANES)
            cur = up[:, half * LANES:(half + 1) * LANES]
            up_ref[s, CONV_HALO:CONV_HALO + tm, :] = cur
            out = cb_ref[:, cols]
            for k in range(CONV_WIDTH - 1):
                lo = CONV_HALO - (CONV_WIDTH - 1) + k
                out = out + cw_ref[k:k + 1, cols] * up_ref[s, lo:lo + tm, :]
            halves.append(out + cw_ref[CONV_WIDTH - 1:CONV_WIDTH, cols] * cur)
        return _cat(halves)

    for j in range(FF_SLABS):
        a = conv(j)
        b = conv(FF_SLABS + j)
        act_ref[:, j * SLAB:(j + 1) * SLAB] = _bf16(_gelu(a) * b)
    up_ref[:, 0:CONV_HALO, :] = up_ref[:, tm:tm + CONV_HALO, :]
    act = act_ref[...]
    x = [_col(x_ref, j) + _dot(act, w_down_ref[j]) for j in range(N_SLABS)]

    x_scale = _rms_scale(x)
    hp = _bf16(_cat([x[j] * x_scale * _col(pg_ref, j) for j in range(N_SLABS)]))
    pb = _bf16(p_ref[...])
    for j in range(N_SLABS):
        gate = _sigmoid(_dot(hp, w_gate_ref[j]))
        x[j] = x[j] + gate * _dot(pb, w_ple_ref[j])
    if final:
        x_scale = _rms_scale(x)
        x = [x[j] * x_scale * _col(fg_ref, j) for j in range(N_SLABS)]
    for j in range(N_SLABS):
        o_ref[:, j * SLAB:(j + 1) * SLAB] = x[j]


def _resident(shape):
    return pl.BlockSpec(shape, lambda i: (0,) * len(shape), pipeline_mode=pl.Buffered(1))


def _compiler_params():
    return pltpu.CompilerParams(dimension_semantics=("arbitrary",),
                                vmem_limit_bytes=V7X_VMEM_LIMIT_BYTES)


def _mixer(x, seq_len, *weights):
    n, d = x.shape
    tm = TOKEN_TILE
    tile = pl.BlockSpec((tm, d), lambda i: (i, 0))
    return pl.pallas_call(
        functools.partial(_mixer_kernel, seq_len),
        grid=(n // tm,),
        in_specs=[tile] + [_resident(w.shape) for w in weights],
        out_specs=tile,
        out_shape=jax.ShapeDtypeStruct((n, d), jnp.float32),
        scratch_shapes=[pltpu.VMEM((d // LANES, tm + POOL_HALO, LANES), jnp.float32)],
        compiler_params=_compiler_params(),
        name="mixer",
    )(x, *weights)


def _ffn(x, p, layer, seq_len, final, *weights):
    n, d = x.shape
    tm = TOKEN_TILE
    steps = n // tm
    tile = pl.BlockSpec((tm, d), lambda i: (i, 0))
    return pl.pallas_call(
        functools.partial(_ffn_kernel, seq_len, final),
        grid=(steps,),
        in_specs=[tile, pl.BlockSpec((tm, PLE_DIM), lambda i: (layer * steps + i, 0))]
        + [_resident(w.shape) for w in weights],
        out_specs=tile,
        out_shape=jax.ShapeDtypeStruct((n, d), jnp.float32),
        scratch_shapes=[pltpu.VMEM((2 * D_FF // LANES, tm + CONV_HALO, LANES), jnp.float32),
                        pltpu.VMEM((tm, D_FF), jnp.bfloat16)],
        compiler_params=_compiler_params(),
        name="ffn_final" if final else "ffn",
    )(x, p, *weights)


def _slabs(w):
    k, n = w.shape
    return _bf16(w).reshape(k, n // SLAB, SLAB).transpose(1, 0, 2)


def kernel(x, p, mix_norm, w_in, w_pool, pool_scale, sgu_norm, w_spatial, b_spatial, w_branch_a, w_branch_b, w_out, ffn_norm, w_up, conv_w, conv_b, w_down, ple_norm, w_ple_gate, w_ple, final_norm):
    batch, seq_len, d = x.shape
    depth = w_in.shape[0]
    assert d == D_MODEL and seq_len % TOKEN_TILE == 0 and seq_len & (seq_len - 1) == 0
    assert TOKEN_TILE % SGU_CHUNK == 0
    n = batch * seq_len
    row = lambda v: v.reshape(1, -1)
    xf = x.reshape(n, d)
    pf = p.reshape(depth * n, PLE_DIM)
    for i in range(depth):
        bs = jnp.repeat(b_spatial[i].T, SGU_HEAD_DIM, axis=1)
        xf = _mixer(xf, seq_len, row(mix_norm[i]), _slabs(w_in[i]), _bf16(w_pool[i]),
                    row(pool_scale[i]), row(sgu_norm[i]), w_spatial[i], bs,
                    _slabs(w_branch_a[i]), _slabs(w_branch_b[i]), _slabs(w_out[i]))
        xf = _ffn(xf, pf, i, seq_len, i == depth - 1, row(ffn_norm[i]),
                  _slabs(w_up[i]), conv_w[i], row(conv_b[i]), _slabs(w_down[i]),
                  row(ple_norm[i]), _slabs(w_ple_gate[i]), _slabs(w_ple[i]), row(final_norm))
    return xf.reshape(batch, seq_len, d)
```

```python
import functools

import jax
import jax.numpy as jnp
import numpy as np
from jax import lax
from jax.experimental import pallas as pl
from jax.experimental.pallas import tpu as pltpu

LANES = 128
SUBLANES = 8
SLAB = 256
D_MODEL = 1024
N_SLABS = D_MODEL // SLAB
POOL_WINDOWS = (2, 4, 8, 16)
POOL_HALO = 16
SGU_CHUNK = 128
SGU_HEADS = 8
SGU_HEAD_DIM = D_MODEL // SGU_HEADS
D_FF = 2816
FF_SLABS = D_FF // SLAB
CONV_WIDTH = 3
CONV_HALO = 8
PLE_DIM = 256
EPS = 1e-6

TOKEN_TILE = 512
V7X_VMEM_LIMIT_BYTES = 56 * 1024 * 1024

_SQRT_HALF = float(np.sqrt(0.5))


def _gelu(x):
    return (0.5 * x) * (1.0 + lax.erf(x * _SQRT_HALF))


def _sigmoid(x):
    return 1.0 / (1.0 + jnp.exp(-x))


def _dot(a, b):
    return jnp.dot(a, b, preferred_element_type=jnp.float32)


def _bf16(x):
    return x.astype(jnp.bfloat16)


def _cat(slabs):
    return jnp.concatenate(slabs, axis=1)


def _rms_scale(slabs):
    ssq = sum(jnp.sum(s * s, axis=-1, keepdims=True) for s in slabs)
    width = sum(s.shape[1] for s in slabs)
    return lax.rsqrt(ssq * (1.0 / width) + EPS)


def _rmsnorm_bf16(x, g_ref):
    return _bf16(x * _rms_scale([x]) * g_ref[...])


def _col(ref, j):
    return ref[:, j * SLAB:(j + 1) * SLAB]


def _mixer_kernel(seq_len, x_ref, g_ref, w_in_ref, w_pool_ref, pscale_ref, sgu_g_ref,
                  ws_ref, bs_ref, wa_ref, wb_ref, wo_ref, o_ref, zp_ref):
    tm = x_ref.shape[0]
    t0 = (pl.program_id(0) * tm) & (seq_len - 1)

    @pl.when(t0 == 0)
    def _():
        zp_ref[:, 0:POOL_HALO, :] = jnp.zeros((2 * N_SLABS, POOL_HALO, LANES), jnp.float32)

    x = x_ref[...]
    h = _rmsnorm_bf16(x, g_ref)

    slab_dots = lambda first: [_dot(h, _col(w_in_ref, first + j)) for j in range(N_SLABS)]

    z = slab_dots(0)
    u = slab_dots(N_SLABS)
    v = slab_dots(2 * N_SLABS)
    t_pos = lax.broadcasted_iota(jnp.int32, (tm, LANES), 0) + t0
    y_pool = []
    for gi, w in enumerate(POOL_WINDOWS):
        inv_cnt = 1.0 / jnp.minimum(t_pos + 1, w).astype(jnp.float32)
        pooled = []
        for half in range(SLAB // LANES):
            s = 2 * gi + half
            z_self = z[gi][:, half * LANES:(half + 1) * LANES]
            zp_ref[s, POOL_HALO:POOL_HALO + tm, :] = z_self
            acc = z_self
            for k in range(1, w):
                acc = acc + zp_ref[s, POOL_HALO - k:POOL_HALO - k + tm, :]
            pooled.append(acc * inv_cnt - z_self)
        y_pool.append(_dot(_bf16(_cat(pooled)), w_pool_ref[gi]) * _col(pscale_ref, gi))
    zp_ref[:, 0:POOL_HALO, :] = zp_ref[:, tm:tm + POOL_HALO, :]
    y_pool = _bf16(_cat(y_pool))
    g_a = slab_dots(3 * N_SLABS)
    y_a = [_dot(y_pool, _col(wa_ref, j)) for j in range(N_SLABS)]

    u = [_gelu(s) for s in u]
    v = [_gelu(s) for s in v]
    v_scale = _rms_scale(v)
    vn = [_bf16(v[j] * v_scale * _col(sgu_g_ref, j)) for j in range(N_SLABS)]
    row = lax.broadcasted_iota(jnp.int32, (SGU_CHUNK, SGU_CHUNK), 0)
    col = lax.broadcasted_iota(jnp.int32, (SGU_CHUNK, SGU_CHUNK), 1)
    tril = row >= col
    ws = [_bf16(jnp.where(tril, ws_ref[hh], 0.0)) for hh in range(SGU_HEADS)]
    mixed = []
    for j in range(N_SLABS):
        chunks = []
        for c in range(tm // SGU_CHUNK):
            rows = slice(c * SGU_CHUNK, (c + 1) * SGU_CHUNK)
            heads = []
            for half in range(SLAB // SGU_HEAD_DIM):
                hcols = slice(half * SGU_HEAD_DIM, (half + 1) * SGU_HEAD_DIM)
                heads.append(_dot(ws[2 * j + half], vn[j][rows, hcols]))
            chunks.append(_cat(heads) + _col(bs_ref, j))
        mixed.append(jnp.concatenate(chunks, axis=0))
    g_b = slab_dots(4 * N_SLABS)
    sgu = _bf16(_cat([u[j] * mixed[j] for j in range(N_SLABS)]))
    y_b = [_dot(sgu, _col(wb_ref, j)) for j in range(N_SLABS)]

    merged = _bf16(_cat([_sigmoid(g_a[j]) * y_a[j] + _sigmoid(g_b[j]) * y_b[j]
                         for j in range(N_SLABS)]))
    for j in range(N_SLABS):
        o_ref[:, j * SLAB:(j + 1) * SLAB] = _col(x_ref, j) + _dot(merged, _col(wo_ref, j))


def _ffn_kernel(seq_len, final, x_ref, p_ref, g_ref, w_up_ref, cw_ref, cb_ref, w_down_ref,
                pg_ref, w_gate_ref, w_ple_ref, fg_ref, o_ref, up_ref, act_ref):
    tm = x_ref.shape[0]
    t0 = (pl.program_id(0) * tm) & (seq_len - 1)

    @pl.when(t0 == 0)
    def _():
        up_ref[:, 0:CONV_HALO, :] = jnp.zeros((up_ref.shape[0], CONV_HALO, LANES), jnp.float32)

    x = x_ref[...]
    h = _rmsnorm_bf16(x, g_ref)

    def conv(slab):
        up = _dot(h, _col(w_up_ref, slab))
        halves = []
        for half in range(SLAB // LANES):
            s = 2 * slab + half
            cols = slice(s * LANES, (s + 1) * LANES)
            cur = up[:, half * LANES:(half + 1) * LANES]
            up_ref[s, CONV_HALO:CONV_HALO + tm, :] = cur
            out = cb_ref[:, cols]
            for k in range(CONV_WIDTH - 1):
                lo = CONV_HALO - (CONV_WIDTH - 1) + k
                out = out + cw_ref[k:k + 1, cols] * up_ref[s, lo:lo + tm, :]
            halves.append(out + cw_ref[CONV_WIDTH - 1:CONV_WIDTH, cols] * cur)
        return _cat(halves)

    for j in range(FF_SLABS):
        a = conv(j)
        b = conv(FF_SLABS + j)
        act_ref[:, j * SLAB:(j + 1) * SLAB] = _bf16(_gelu(a) * b)
    up_ref[:, 0:CONV_HALO, :] = up_ref[:, tm:tm + CONV_HALO, :]
    act = act_ref[...]
    x = [_col(x_ref, j) + _dot(act, _col(w_down_ref, j)) for j in range(N_SLABS)]

    x_scale = _rms_scale(x)
    hp = _bf16(_cat([x[j] * x_scale * _col(pg_ref, j) for j in range(N_SLABS)]))
    pb = _bf16(p_ref[...])
    for j in range(N_SLABS):
        gate = _sigmoid(_dot(hp, _col(w_gate_ref, j)))
        x[j] = x[j] + gate * _dot(pb, _col(w_ple_ref, j))
    if final:
        x_scale = _rms_scale(x)
        x = [x[j] * x_scale * _col(fg_ref, j) for j in range(N_SLABS)]
    for j in range(N_SLABS):
        o_ref[:, j * SLAB:(j + 1) * SLAB] = x[j]


def _resident(shape):
    return pl.BlockSpec(shape, lambda i: (0,) * len(shape), pipeline_mode=pl.Buffered(1))


def _compiler_params():
    return pltpu.CompilerParams(dimension_semantics=("arbitrary",),
                                vmem_limit_bytes=V7X_VMEM_LIMIT_BYTES)


def _mixer(x, seq_len, *weights):
    n, d = x.shape
    tm = TOKEN_TILE
    tile = pl.BlockSpec((tm, d), lambda i: (i, 0))
    return pl.pallas_call(
        functools.partial(_mixer_kernel, seq_len),
        grid=(n // tm,),
        in_specs=[tile] + [_resident(w.shape) for w in weights],
        out_specs=tile,
        out_shape=jax.ShapeDtypeStruct((n, d), jnp.float32),
        scratch_shapes=[pltpu.VMEM((d // LANES, tm + POOL_HALO, LANES), jnp.float32)],
        compiler_params=_compiler_params(),
        name="mixer",
    )(x, *weights)


def _ffn(x, p, layer, seq_len, final, *weights):
    n, d = x.shape
    tm = TOKEN_TILE
    steps = n // tm
    tile = pl.BlockSpec((tm, d), lambda i: (i, 0))
    return pl.pallas_call(
        functools.partial(_ffn_kernel, seq_len, final),
        grid=(steps,),
        in_specs=[tile, pl.BlockSpec((tm, PLE_DIM), lambda i: (layer * steps + i, 0))]
        + [_resident(w.shape) for w in weights],
        out_specs=tile,
        out_shape=jax.ShapeDtypeStruct((n, d), jnp.float32),
        scratch_shapes=[pltpu.VMEM((2 * D_FF // LANES, tm + CONV_HALO, LANES), jnp.float32),
                        pltpu.VMEM((tm, D_FF), jnp.bfloat16)],
        compiler_params=_compiler_params(),
        name="ffn_final" if final else "ffn",
    )(x, p, *weights)


def _weight(w):
    k, n = w.shape
    pad = LANES if (n // LANES) % SUBLANES == 0 else 0
    return jnp.pad(_bf16(w), ((0, 0), (0, pad)))


def kernel(x, p, mix_norm, w_in, w_pool, pool_scale, sgu_norm, w_spatial, b_spatial, w_branch_a, w_branch_b, w_out, ffn_norm, w_up, conv_w, conv_b, w_down, ple_norm, w_ple_gate, w_ple, final_norm):
    batch, seq_len, d = x.shape
    depth = w_in.shape[0]
    assert d == D_MODEL and seq_len % TOKEN_TILE == 0 and seq_len & (seq_len - 1) == 0
    assert TOKEN_TILE % SGU_CHUNK == 0
    n = batch * seq_len
    row = lambda v: v.reshape(1, -1)
    xf = x.reshape(n, d)
    pf = p.reshape(depth * n, PLE_DIM)
    for i in range(depth):
        bs = jnp.repeat(b_spatial[i].T, SGU_HEAD_DIM, axis=1)
        xf = _mixer(xf, seq_len, row(mix_norm[i]), _weight(w_in[i]), _bf16(w_pool[i]),
                    row(pool_scale[i]), row(sgu_norm[i]), w_spatial[i], bs,
                    _weight(w_branch_a[i]), _weight(w_branch_b[i]), _weight(w_out[i]))
        xf = _ffn(xf, pf, i, seq_len, i == depth - 1, row(ffn_norm[i]),
                  _weight(w_up[i]), conv_w[i], row(conv_b[i]), _weight(w_down[i]),
                  row(ple_norm[i]), _weight(w_ple_gate[i]), _weight(w_ple[i]), row(final_norm))
    return xf.reshape(batch, seq_len, d)
```

```python
import functools

import jax
import jax.numpy as jnp
import numpy as np
from jax import lax
from jax.experimental import pallas as pl
from jax.experimental.pallas import tpu as pltpu

LANES = 128
SUBLANES = 8
SLAB = 256
D_MODEL = 1024
N_SLABS = D_MODEL // SLAB
POOL_WINDOWS = (2, 4, 8, 16)
POOL_HALO = 16
SGU_CHUNK = 128
SGU_HEADS = 8
SGU_HEAD_DIM = D_MODEL // SGU_HEADS
D_FF = 2816
FF_SLABS = D_FF // SLAB
CONV_WIDTH = 3
CONV_HALO = 8
PLE_DIM = 256
EPS = 1e-6

TOKEN_TILE = 512
V7X_VMEM_LIMIT_BYTES = 56 * 1024 * 1024

_SQRT_HALF = float(np.sqrt(0.5))


def _gelu(x):
    return (0.5 * x) * (1.0 + lax.erf(x * _SQRT_HALF))


def _sigmoid(x):
    return 1.0 / (1.0 + jnp.exp(-x))


def _dot(a, b):
    return jnp.dot(a, b, preferred_element_type=jnp.float32)


def _bf16(x):
    return x.astype(jnp.bfloat16)


def _cat(slabs):
    return jnp.concatenate(slabs, axis=1)


def _rms_scale(slabs):
    ssq = sum(jnp.sum(s * s, axis=-1, keepdims=True) for s in slabs)
    width = sum(s.shape[1] for s in slabs)
    return lax.rsqrt(ssq * (1.0 / width) + EPS)


def _rmsnorm_bf16(x, g_ref):
    return _bf16(x * _rms_scale([x]) * g_ref[...])


def _col(ref, j):
    return ref[:, j * SLAB:(j + 1) * SLAB]


def _mixer_kernel(seq_len, x_ref, g_ref, w_in_ref, w_pool_ref, pscale_ref, sgu_g_ref,
                  ws_ref, bs_ref, wa_ref, wb_ref, wo_ref, o_ref, zp_ref):
    tm = x_ref.shape[0]
    t0 = (pl.program_id(0) * tm) & (seq_len - 1)

    @pl.when(t0 == 0)
    def _():
        zp_ref[:, 0:POOL_HALO, :] = jnp.zeros((2 * N_SLABS, POOL_HALO, LANES), jnp.float32)

    x = x_ref[...]
    h = _rmsnorm_bf16(x, g_ref)

    slab_dots = lambda first: [_dot(h, _col(w_in_ref, first + j)) for j in range(N_SLABS)]

    z = slab_dots(0)
    u = slab_dots(N_SLABS)
    v = slab_dots(2 * N_SLABS)
    t_pos = lax.broadcasted_iota(jnp.int32, (tm, LANES), 0) + t0
    y_pool = []
    for gi, w in enumerate(POOL_WINDOWS):
        inv_cnt = 1.0 / jnp.minimum(t_pos + 1, w).astype(jnp.float32)
        pooled = []
        for half in range(SLAB // LANES):
            s = 2 * gi + half
            z_self = z[gi][:, half * LANES:(half + 1) * LANES]
            zp_ref[s, POOL_HALO:POOL_HALO + tm, :] = z_self
            acc = z_self
            for k in range(1, w):
                acc = acc + zp_ref[s, POOL_HALO - k:POOL_HALO - k + tm, :]
            pooled.append(acc * inv_cnt - z_self)
        y_pool.append(_dot(_bf16(_cat(pooled)), w_pool_ref[gi]) * _col(pscale_ref, gi))
    zp_ref[:, 0:POOL_HALO, :] = zp_ref[:, tm:tm + POOL_HALO, :]
    y_pool = _bf16(_cat(y_pool))
    g_a = slab_dots(3 * N_SLABS)
    y_a = [_dot(y_pool, _col(wa_ref, j)) for j in range(N_SLABS)]

    u = [_gelu(s) for s in u]
    v = [_gelu(s) for s in v]
    v_scale = _rms_scale(v)
    vn = [_bf16(v[j] * v_scale * _col(sgu_g_ref, j)) for j in range(N_SLABS)]
    row = lax.broadcasted_iota(jnp.int32, (SGU_CHUNK, SGU_CHUNK), 0)
    col = lax.broadcasted_iota(jnp.int32, (SGU_CHUNK, SGU_CHUNK), 1)
    tril = row >= col
    ws = [_bf16(jnp.where(tril, ws_ref[hh], 0.0)) for hh in range(SGU_HEADS)]
    mixed = []
    for j in range(N_SLABS):
        chunks = []
        for c in range(tm // SGU_CHUNK):
            rows = slice(c * SGU_CHUNK, (c + 1) * SGU_CHUNK)
            heads = []
            for half in range(SLAB // SGU_HEAD_DIM):
                hcols = slice(half * SGU_HEAD_DIM, (half + 1) * SGU_HEAD_DIM)
                heads.append(_dot(ws[2 * j + half], vn[j][rows, hcols]))
            chunks.append(_cat(heads) + _col(bs_ref, j))
        mixed.append(jnp.concatenate(chunks, axis=0))
    g_b = slab_dots(4 * N_SLABS)
    sgu = _bf16(_cat([u[j] * mixed[j] for j in range(N_SLABS)]))
    y_b = [_dot(sgu, _col(wb_ref, j)) for j in range(N_SLABS)]

    merged = _bf16(_cat([_sigmoid(g_a[j]) * y_a[j] + _sigmoid(g_b[j]) * y_b[j]
                         for j in range(N_SLABS)]))
    for j in range(N_SLABS):
        o_ref[:, j * SLAB:(j + 1) * SLAB] = _col(x_ref, j) + _dot(merged, _col(wo_ref, j))


def _ffn_kernel(seq_len, final, x_ref, p_ref, g_ref, w_up_ref, cw_ref, cb_ref, w_down_ref,
                pg_ref, w_gate_ref, w_ple_ref, fg_ref, o_ref, up_ref, act_ref):
    tm = x_ref.shape[0]
    t0 = (pl.program_id(0) * tm) & (seq_len - 1)

    @pl.when(t0 == 0)
    def _():
        up_ref[:, 0:CONV_HALO, :] = jnp.zeros((up_ref.shape[0], CONV_HALO, LANES), jnp.float32)

    x = x_ref[...]
    h = _rmsnorm_bf16(x, g_ref)

    def conv(slab):
        up = _dot(h, _col(w_up_ref, slab))
        halves = []
        for half in range(SLAB // LANES):
            s = 2 * slab + half
            cols = slice(s * LANES, (s + 1) * LANES)
            cur = up[:, half * LANES:(half + 1) * LANES]
            up_ref[s, CONV_HALO:CONV_HALO + tm, :] = cur
            out = cb_ref[:, cols]
            for k in range(CONV_WIDTH - 1):
                lo = CONV_HALO - (CONV_WIDTH - 1) + k
                out = out + cw_ref[k:k + 1, cols] * up_ref[s, lo:lo + tm, :]
            halves.append(out + cw_ref[CONV_WIDTH - 1:CONV_WIDTH, cols] * cur)
        return _cat(halves)

    for j in range(FF_SLABS):
        a = conv(j)
        b = conv(FF_SLABS + j)
        act_ref[:, j * SLAB:(j + 1) * SLAB] = _bf16(_gelu(a) * b)
    up_ref[:, 0:CONV_HALO, :] = up_ref[:, tm:tm + CONV_HALO, :]
    act = act_ref[...]
    x = [_col(x_ref, j) + _dot(act, _col(w_down_ref, j)) for j in range(N_SLABS)]

    x_scale = _rms_scale(x)
    hp = _bf16(_cat([x[j] * x_scale * _col(pg_ref, j) for j in range(N_SLABS)]))
    pb = _bf16(p_ref[...])
    for j in range(N_SLABS):
        gate = _sigmoid(_dot(hp, _col(w_gate_ref, j)))
        x[j] = x[j] + gate * _dot(pb, _col(w_ple_ref, j))
    if final:
        x_scale = _rms_scale(x)
        x = [x[j] * x_scale * _col(fg_ref, j) for j in range(N_SLABS)]
    for j in range(N_SLABS):
        o_ref[:, j * SLAB:(j + 1) * SLAB] = x[j]


def _layer_block(w, layer):
    rest = w.shape[1:]
    return pl.BlockSpec((None,) + rest, lambda i: (layer,) + (0,) * len(rest),
                        pipeline_mode=pl.Buffered(1))


def _compiler_params():
    return pltpu.CompilerParams(dimension_semantics=("arbitrary",),
                                vmem_limit_bytes=V7X_VMEM_LIMIT_BYTES)


def _mixer(x, layer, seq_len, *weights):
    n, d = x.shape
    tm = TOKEN_TILE
    tile = pl.BlockSpec((tm, d), lambda i: (i, 0))
    return pl.pallas_call(
        functools.partial(_mixer_kernel, seq_len),
        grid=(n // tm,),
        in_specs=[tile] + [_layer_block(w, layer) for w in weights],
        out_specs=tile,
        out_shape=jax.ShapeDtypeStruct((n, d), jnp.float32),
        scratch_shapes=[pltpu.VMEM((d // LANES, tm + POOL_HALO, LANES), jnp.float32)],
        compiler_params=_compiler_params(),
        name="mixer",
    )(x, *weights)


def _ffn(x, p, layer, seq_len, final, final_gain, *weights):
    n, d = x.shape
    tm = TOKEN_TILE
    steps = n // tm
    tile = pl.BlockSpec((tm, d), lambda i: (i, 0))
    return pl.pallas_call(
        functools.partial(_ffn_kernel, seq_len, final),
        grid=(steps,),
        in_specs=[tile, pl.BlockSpec((tm, PLE_DIM), lambda i: (layer * steps + i, 0))]
        + [_layer_block(w, layer) for w in weights]
        + [pl.BlockSpec(final_gain.shape, lambda i: (0, 0))],
        out_specs=tile,
        out_shape=jax.ShapeDtypeStruct((n, d), jnp.float32),
        scratch_shapes=[pltpu.VMEM((2 * D_FF // LANES, tm + CONV_HALO, LANES), jnp.float32),
                        pltpu.VMEM((tm, D_FF), jnp.bfloat16)],
        compiler_params=_compiler_params(),
        name="ffn_final" if final else "ffn",
    )(x, p, *weights, final_gain)


def _weight(w):
    pad = LANES if (w.shape[-1] // LANES) % SUBLANES == 0 else 0
    return _bf16(jnp.pad(w, ((0, 0), (0, 0), (0, pad))))


def kernel(x, p, mix_norm, w_in, w_pool, pool_scale, sgu_norm, w_spatial, b_spatial, w_branch_a, w_branch_b, w_out, ffn_norm, w_up, conv_w, conv_b, w_down, ple_norm, w_ple_gate, w_ple, final_norm):
    batch, seq_len, d = x.shape
    depth = w_in.shape[0]
    assert d == D_MODEL and seq_len % TOKEN_TILE == 0 and seq_len & (seq_len - 1) == 0
    assert TOKEN_TILE % SGU_CHUNK == 0
    n = batch * seq_len
    row = lambda v: v.reshape(depth, 1, -1)
    bs = jnp.repeat(jnp.swapaxes(b_spatial, 1, 2), SGU_HEAD_DIM, axis=2)
    mixer_params = (row(mix_norm), _weight(w_in), _bf16(w_pool), row(pool_scale), row(sgu_norm),
                    w_spatial, bs, _weight(w_branch_a), _weight(w_branch_b), _weight(w_out))
    ffn_params = (row(ffn_norm), _weight(w_up), conv_w, row(conv_b), _weight(w_down),
                  row(ple_norm), _weight(w_ple_gate), _weight(w_ple))
    xf = x.reshape(n, d)
    pf = p.reshape(depth * n, PLE_DIM)
    for i in range(depth):
        xf = _mixer(xf, i, seq_len, *mixer_params)
        xf = _ffn(xf, pf, i, seq_len, i == depth - 1, final_norm.reshape(1, d), *ffn_params)
    return xf.reshape(batch, seq_len, d)
```

```python
import functools

import jax
import jax.numpy as jnp
import numpy as np
from jax import lax
from jax.experimental import pallas as pl
from jax.experimental.pallas import tpu as pltpu

LANES = 128
SUBLANES = 8
BF16_ROWS = 16
SLAB = 256
D_MODEL = 1024
N_SLABS = D_MODEL // SLAB
POOL_WINDOWS = (2, 4, 8, 16)
POOL_GROUP_WIDTH = SLAB
POOL_HALO = 16
SGU_CHUNK = 128
SGU_HEADS = 8
SGU_HEAD_DIM = D_MODEL // SGU_HEADS
D_FF = 2816
FF_SLABS = D_FF // SLAB
CONV_WIDTH = 3
CONV_HALO = 8
PLE_DIM = 256
EPS = 1e-6

TOKEN_TILE = 512
V7X_VMEM_LIMIT_BYTES = 56 * 1024 * 1024

_SQRT_HALF = float(np.sqrt(0.5))
_N_MIXER_IN = 11
_N_FFN_IN = 11


def _gelu(x):
    return (0.5 * x) * (1.0 + lax.erf(x * _SQRT_HALF))


def _sigmoid(x):
    return 1.0 / (1.0 + jnp.exp(-x))


def _dot(a, b):
    return jnp.dot(a, b, preferred_element_type=jnp.float32)


def _bf16(x):
    return x.astype(jnp.bfloat16)


def _cat(slabs):
    return jnp.concatenate(slabs, axis=1)


def _rms_scale(slabs):
    ssq = sum(jnp.sum(s * s, axis=-1, keepdims=True) for s in slabs)
    width = sum(s.shape[1] for s in slabs)
    return lax.rsqrt(ssq * (1.0 / width) + EPS)


def _rmsnorm_bf16(x, g_ref):
    return _bf16(x * _rms_scale([x]) * g_ref[...])


def _col(ref, j):
    return ref[:, j * SLAB:(j + 1) * SLAB]


def _cast_rows(src_refs, dst_refs):
    for src, dst in zip(src_refs, dst_refs):
        rows, n = src.shape
        dst[:, 0:n] = _bf16(src[...])
        if dst.shape[1] > n:
            dst[:, n:] = jnp.zeros((rows, dst.shape[1] - n), jnp.bfloat16)


def _mixer_kernel(seq_len, n_cast, *refs):
    (x_ref, g_ref, w_in_ref, w_pool_ref, pscale_ref, sgu_g_ref, ws_ref, bs_ref,
     wa_ref, wb_ref, wo_ref) = refs[:_N_MIXER_IN]
    cast_in = refs[_N_MIXER_IN:_N_MIXER_IN + n_cast]
    o_ref = refs[_N_MIXER_IN + n_cast]
    cast_out = refs[_N_MIXER_IN + n_cast + 1:_N_MIXER_IN + 2 * n_cast + 1]
    (zp_ref,) = refs[_N_MIXER_IN + 2 * n_cast + 1:]
    tm = x_ref.shape[0]
    t0 = (pl.program_id(0) * tm) & (seq_len - 1)

    @pl.when(t0 == 0)
    def _():
        zp_ref[:, 0:POOL_HALO, :] = jnp.zeros((2 * N_SLABS, POOL_HALO, LANES), jnp.float32)

    x = x_ref[...]
    h = _rmsnorm_bf16(x, g_ref)

    slab_dots = lambda first: [_dot(h, _col(w_in_ref, first + j)) for j in range(N_SLABS)]

    z = slab_dots(0)
    u = slab_dots(N_SLABS)
    v = slab_dots(2 * N_SLABS)
    t_pos = lax.broadcasted_iota(jnp.int32, (tm, LANES), 0) + t0
    y_pool = []
    for gi, w in enumerate(POOL_WINDOWS):
        inv_cnt = 1.0 / jnp.minimum(t_pos + 1, w).astype(jnp.float32)
        pooled = []
        for half in range(SLAB // LANES):
            s = 2 * gi + half
            z_self = z[gi][:, half * LANES:(half + 1) * LANES]
            zp_ref[s, POOL_HALO:POOL_HALO + tm, :] = z_self
            acc = z_self
            for k in range(1, w):
                acc = acc + zp_ref[s, POOL_HALO - k:POOL_HALO - k + tm, :]
            pooled.append(acc * inv_cnt - z_self)
        y_pool.append(_dot(_bf16(_cat(pooled)), w_pool_ref[gi]) * _col(pscale_ref, gi))
    zp_ref[:, 0:POOL_HALO, :] = zp_ref[:, tm:tm + POOL_HALO, :]
    y_pool = _bf16(_cat(y_pool))
    g_a = slab_dots(3 * N_SLABS)
    y_a = [_dot(y_pool, _col(wa_ref, j)) for j in range(N_SLABS)]

    u = [_gelu(s) for s in u]
    v = [_gelu(s) for s in v]
    v_scale = _rms_scale(v)
    vn = [_bf16(v[j] * v_scale * _col(sgu_g_ref, j)) for j in range(N_SLABS)]
    row = lax.broadcasted_iota(jnp.int32, (SGU_CHUNK, SGU_CHUNK), 0)
    col = lax.broadcasted_iota(jnp.int32, (SGU_CHUNK, SGU_CHUNK), 1)
    tril = row >= col
    ws = [_bf16(jnp.where(tril, ws_ref[hh], 0.0)) for hh in range(SGU_HEADS)]
    mixed = []
    for j in range(N_SLABS):
        chunks = []
        for c in range(tm // SGU_CHUNK):
            rows = slice(c * SGU_CHUNK, (c + 1) * SGU_CHUNK)
            heads = []
            for half in range(SLAB // SGU_HEAD_DIM):
                hcols = slice(half * SGU_HEAD_DIM, (half + 1) * SGU_HEAD_DIM)
                heads.append(_dot(ws[2 * j + half], vn[j][rows, hcols]))
            chunks.append(_cat(heads) + _col(bs_ref, j))
        mixed.append(jnp.concatenate(chunks, axis=0))
    g_b = slab_dots(4 * N_SLABS)
    sgu = _bf16(_cat([u[j] * mixed[j] for j in range(N_SLABS)]))
    y_b = [_dot(sgu, _col(wb_ref, j)) for j in range(N_SLABS)]

    merged = _bf16(_cat([_sigmoid(g_a[j]) * y_a[j] + _sigmoid(g_b[j]) * y_b[j]
                         for j in range(N_SLABS)]))
    for j in range(N_SLABS):
        o_ref[:, j * SLAB:(j + 1) * SLAB] = _col(x_ref, j) + _dot(merged, _col(wo_ref, j))
    _cast_rows(cast_in, cast_out)


def _ffn_kernel(seq_len, final, n_cast, *refs):
    (x_ref, p_ref, g_ref, w_up_ref, cw_ref, cb_ref, w_down_ref, pg_ref, w_gate_ref,
     w_ple_ref, fg_ref) = refs[:_N_FFN_IN]
    cast_in = refs[_N_FFN_IN:_N_FFN_IN + n_cast]
    o_ref = refs[_N_FFN_IN + n_cast]
    cast_out = refs[_N_FFN_IN + n_cast + 1:_N_FFN_IN + 2 * n_cast + 1]
    up_ref, act_ref = refs[_N_FFN_IN + 2 * n_cast + 1:]
    tm = x_ref.shape[0]
    t0 = (pl.program_id(0) * tm) & (seq_len - 1)

    @pl.when(t0 == 0)
    def _():
        up_ref[:, 0:CONV_HALO, :] = jnp.zeros((up_ref.shape[0], CONV_HALO, LANES), jnp.float32)

    x = x_ref[...]
    h = _rmsnorm_bf16(x, g_ref)

    def conv(slab):
        up = _dot(h, _col(w_up_ref, slab))
        halves = []
        for half in range(SLAB // LANES):
            s = 2 * slab + half
            cols = slice(s * LANES, (s + 1) * LANES)
            cur = up[:, half * LANES:(half + 1) * LANES]
            up_ref[s, CONV_HALO:CONV_HALO + tm, :] = cur
            out = cb_ref[:, cols]
            for k in range(CONV_WIDTH - 1):
                lo = CONV_HALO - (CONV_WIDTH - 1) + k
                out = out + cw_ref[k:k + 1, cols] * up_ref[s, lo:lo + tm, :]
            halves.append(out + cw_ref[CONV_WIDTH - 1:CONV_WIDTH, cols] * cur)
        return _cat(halves)

    for j in range(FF_SLABS):
        a = conv(j)
        b = conv(FF_SLABS + j)
        act_ref[:, j * SLAB:(j + 1) * SLAB] = _bf16(_gelu(a) * b)
    up_ref[:, 0:CONV_HALO, :] = up_ref[:, tm:tm + CONV_HALO, :]
    act = act_ref[...]
    x = [_col(x_ref, j) + _dot(act, _col(w_down_ref, j)) for j in range(N_SLABS)]

    x_scale = _rms_scale(x)
    hp = _bf16(_cat([x[j] * x_scale * _col(pg_ref, j) for j in range(N_SLABS)]))
    pb = _bf16(p_ref[...])
    for j in range(N_SLABS):
        gate = _sigmoid(_dot(hp, _col(w_gate_ref, j)))
        x[j] = x[j] + gate * _dot(pb, _col(w_ple_ref, j))
    if final:
        x_scale = _rms_scale(x)
        x = [x[j] * x_scale * _col(fg_ref, j) for j in range(N_SLABS)]
    for j in range(N_SLABS):
        o_ref[:, j * SLAB:(j + 1) * SLAB] = x[j]
    _cast_rows(cast_in, cast_out)


def _layer_block(w, layer):
    rest = w.shape[1:]
    return pl.BlockSpec((None,) + rest, lambda i: (layer,) + (0,) * len(rest),
                        pipeline_mode=pl.Buffered(1))


def _whole(w):
    return pl.BlockSpec(w.shape, lambda i: (0,) * w.ndim, pipeline_mode=pl.Buffered(1))


def _padded_width(n):
    return n + (LANES if (n // LANES) % SUBLANES == 0 else 0)


def _cast_plan(w, layer, steps):
    _, k, n = w.shape
    rows = next(r for r in range(BF16_ROWS, k + 1, BF16_ROWS) if k % r == 0 and k // r <= steps)
    last = k // rows - 1
    npad = _padded_width(n)
    return (pl.BlockSpec((None, rows, n), lambda i: (layer, jnp.minimum(i, last), 0)),
            pl.BlockSpec((rows, npad), lambda i: (jnp.minimum(i, last), 0)),
            jax.ShapeDtypeStruct((k, npad), jnp.bfloat16))


def _compiler_params():
    return pltpu.CompilerParams(dimension_semantics=("arbitrary",),
                                vmem_limit_bytes=V7X_VMEM_LIMIT_BYTES)


def _call(body, name, x, tile_inputs, tile_specs, params, param_specs, to_cast, cast_layer,
          scratch):
    n, d = x.shape
    tm = TOKEN_TILE
    steps = n // tm
    tile = pl.BlockSpec((tm, d), lambda i: (i, 0))
    plans = [_cast_plan(w, cast_layer, steps) for w in to_cast]
    out = pl.pallas_call(
        functools.partial(body, len(plans)),
        grid=(steps,),
        in_specs=[tile] + tile_specs + param_specs + [pl_in for pl_in, _, _ in plans],
        out_specs=[tile] + [pl_out for _, pl_out, _ in plans],
        out_shape=[jax.ShapeDtypeStruct((n, d), jnp.float32)] + [s for _, _, s in plans],
        scratch_shapes=scratch,
        compiler_params=_compiler_params(),
        name=name,
    )(x, *tile_inputs, *params, *to_cast)
    return out[0], out[1:]


def _weight(w):
    return _bf16(jnp.pad(w, ((0, 0), (0, _padded_width(w.shape[1]) - w.shape[1]))))


def kernel(x, p, mix_norm, w_in, w_pool, pool_scale, sgu_norm, w_spatial, b_spatial, w_branch_a, w_branch_b, w_out, ffn_norm, w_up, conv_w, conv_b, w_down, ple_norm, w_ple_gate, w_ple, final_norm):
    batch, seq_len, d = x.shape
    depth = w_in.shape[0]
    tm = TOKEN_TILE
    assert d == D_MODEL and seq_len % tm == 0 and seq_len & (seq_len - 1) == 0
    assert tm % SGU_CHUNK == 0
    n = batch * seq_len
    steps = n // tm
    row = lambda v: v.reshape(depth, 1, -1)
    bs = jnp.repeat(jnp.swapaxes(b_spatial, 1, 2), SGU_HEAD_DIM, axis=2)
    w_pool_rows = w_pool.reshape(depth, D_MODEL, POOL_GROUP_WIDTH)
    mixer_f32 = (w_in, w_pool_rows, w_branch_a, w_branch_b, w_out)
    ffn_f32 = (w_up, w_down, w_ple_gate, w_ple)
    mixer_scratch = [pltpu.VMEM((d // LANES, tm + POOL_HALO, LANES), jnp.float32)]
    ffn_scratch = [pltpu.VMEM((2 * D_FF // LANES, tm + CONV_HALO, LANES), jnp.float32),
                   pltpu.VMEM((tm, D_FF), jnp.bfloat16)]
    xf = x.reshape(n, d)
    pf = p.reshape(depth * n, PLE_DIM)
    fg = final_norm.reshape(1, d)
    mixer_bf16 = [_weight(w[0]) for w in mixer_f32]
    for i in range(depth):
        last = i == depth - 1
        wi, wp, wa, wb, wo = mixer_bf16
        wp = wp.reshape(len(POOL_WINDOWS), POOL_GROUP_WIDTH, POOL_GROUP_WIDTH)
        params = (row(mix_norm), wi, wp, row(pool_scale), row(sgu_norm), w_spatial, bs, wa, wb, wo)
        specs = [_whole(q) if q.dtype == jnp.bfloat16 else _layer_block(q, i) for q in params]
        xf, (wu, wd, wg, wl) = _call(
            functools.partial(_mixer_kernel, seq_len), "mixer", xf, (), [], params, specs,
            ffn_f32, i, mixer_scratch)
        params = (row(ffn_norm), wu, conv_w, row(conv_b), wd, row(ple_norm), wg, wl, fg)
        specs = [_whole(q) if q.dtype == jnp.bfloat16 or q is fg else _layer_block(q, i)
                 for q in params]
        p_tile = pl.BlockSpec((tm, PLE_DIM), lambda j, i=i: (i * steps + j, 0))
        xf, mixer_bf16 = _call(
            functools.partial(_ffn_kernel, seq_len, last), "ffn_final" if last else "ffn",
            xf, (pf,), [p_tile], params, specs, () if last else mixer_f32, i + 1, ffn_scratch)
    return xf.reshape(batch, seq_len, d)
```

```python
import functools

import jax
import jax.numpy as jnp
import numpy as np
from jax import lax
from jax.experimental import pallas as pl
from jax.experimental.pallas import tpu as pltpu

LANES = 128
SUBLANES = 8
BF16_ROWS = 16
SLAB = 256
D_MODEL = 1024
N_SLABS = D_MODEL // SLAB
POOL_WINDOWS = (2, 4, 8, 16)
POOL_GROUP_WIDTH = SLAB
POOL_HALO = 16
SGU_CHUNK = 128
SGU_HEADS = 8
SGU_HEAD_DIM = D_MODEL // SGU_HEADS
D_FF = 2816
FF_SLABS = D_FF // SLAB
CONV_WIDTH = 3
CONV_HALO = 8
PLE_DIM = 256
EPS = 1e-6

TOKEN_TILE = 512
V7X_VMEM_LIMIT_BYTES = 56 * 1024 * 1024

_SQRT_HALF = float(np.sqrt(0.5))
_N_MIXER_IN = 11
_N_FFN_IN = 11


def _gelu(x):
    return (0.5 * x) * (1.0 + lax.erf(x * _SQRT_HALF))


def _sigmoid(x):
    return 1.0 / (1.0 + jnp.exp(-x))


def _dot(a, b):
    return jnp.dot(a, b, preferred_element_type=jnp.float32)


def _bf16(x):
    return x.astype(jnp.bfloat16)


def _cat(slabs):
    return jnp.concatenate(slabs, axis=1)


def _rms_scale(slabs):
    ssq = sum(jnp.sum(s * s, axis=-1, keepdims=True) for s in slabs)
    width = sum(s.shape[1] for s in slabs)
    return lax.rsqrt(ssq * (1.0 / width) + EPS)


def _rmsnorm_bf16(x, g_ref):
    return _bf16(x * _rms_scale([x]) * g_ref[...])


def _col(ref, j):
    return ref[:, j * SLAB:(j + 1) * SLAB]


def _cast_rows(src_refs, dst_refs):
    for src, dst in zip(src_refs, dst_refs):
        rows, n = src.shape
        dst[:, 0:n] = _bf16(src[...])
        if dst.shape[1] > n:
            dst[:, n:] = jnp.zeros((rows, dst.shape[1] - n), jnp.bfloat16)


def _mixer_kernel(seq_len, n_cast, *refs):
    (x_ref, g_ref, w_in_ref, w_pool_ref, pscale_ref, sgu_g_ref, ws_ref, bs_ref,
     wa_ref, wb_ref, wo_ref) = refs[:_N_MIXER_IN]
    cast_in = refs[_N_MIXER_IN:_N_MIXER_IN + n_cast]
    o_ref = refs[_N_MIXER_IN + n_cast]
    cast_out = refs[_N_MIXER_IN + n_cast + 1:_N_MIXER_IN + 2 * n_cast + 1]
    (zp_ref,) = refs[_N_MIXER_IN + 2 * n_cast + 1:]
    tm = x_ref.shape[0]
    t0 = (pl.program_id(0) * tm) & (seq_len - 1)

    @pl.when(t0 == 0)
    def _():
        zp_ref[:, 0:POOL_HALO, :] = jnp.zeros((2 * N_SLABS, POOL_HALO, LANES), jnp.float32)

    x = x_ref[...]
    rs = _rms_scale([x])
    xg = _bf16(x * g_ref[...])

    slab_dots = lambda first: [rs * _dot(xg, _col(w_in_ref, first + j))
                               for j in range(N_SLABS)]

    z = slab_dots(0)
    u = slab_dots(N_SLABS)
    v = slab_dots(2 * N_SLABS)
    t_pos = lax.broadcasted_iota(jnp.int32, (tm, LANES), 0) + t0
    y_pool = []
    for gi, w in enumerate(POOL_WINDOWS):
        inv_cnt = 1.0 / jnp.minimum(t_pos + 1, w).astype(jnp.float32)
        pooled = []
        for half in range(SLAB // LANES):
            s = 2 * gi + half
            z_self = z[gi][:, half * LANES:(half + 1) * LANES]
            zp_ref[s, POOL_HALO:POOL_HALO + tm, :] = z_self
            acc = z_self
            for k in range(1, w):
                acc = acc + zp_ref[s, POOL_HALO - k:POOL_HALO - k + tm, :]
            pooled.append(acc * inv_cnt - z_self)
        y_pool.append(_dot(_bf16(_cat(pooled)), w_pool_ref[gi]) * _col(pscale_ref, gi))
    zp_ref[:, 0:POOL_HALO, :] = zp_ref[:, tm:tm + POOL_HALO, :]
    y_pool = _bf16(_cat(y_pool))
    g_a = slab_dots(3 * N_SLABS)
    y_a = [_dot(y_pool, _col(wa_ref, j)) for j in range(N_SLABS)]

    u = [_gelu(s) for s in u]
    v = [_gelu(s) for s in v]
    v_scale = _rms_scale(v)
    vn = [_bf16(v[j] * v_scale * _col(sgu_g_ref, j)) for j in range(N_SLABS)]
    row = lax.broadcasted_iota(jnp.int32, (SGU_CHUNK, SGU_CHUNK), 0)
    col = lax.broadcasted_iota(jnp.int32, (SGU_CHUNK, SGU_CHUNK), 1)
    tril = row >= col
    ws = [_bf16(jnp.where(tril, ws_ref[hh], 0.0)) for hh in range(SGU_HEADS)]
    mixed = []
    for j in range(N_SLABS):
        chunks = []
        for c in range(tm // SGU_CHUNK):
            rows = slice(c * SGU_CHUNK, (c + 1) * SGU_CHUNK)
            heads = []
            for half in range(SLAB // SGU_HEAD_DIM):
                hcols = slice(half * SGU_HEAD_DIM, (half + 1) * SGU_HEAD_DIM)
                heads.append(_dot(ws[2 * j + half], vn[j][rows, hcols]))
            chunks.append(_cat(heads) + _col(bs_ref, j))
        mixed.append(jnp.concatenate(chunks, axis=0))
    g_b = slab_dots(4 * N_SLABS)
    sgu = _bf16(_cat([u[j] * mixed[j] for j in range(N_SLABS)]))
    y_b = [_dot(sgu, _col(wb_ref, j)) for j in range(N_SLABS)]

    merged = _bf16(_cat([_sigmoid(g_a[j]) * y_a[j] + _sigmoid(g_b[j]) * y_b[j]
                         for j in range(N_SLABS)]))
    for j in range(N_SLABS):
        o_ref[:, j * SLAB:(j + 1) * SLAB] = _col(x_ref, j) + _dot(merged, _col(wo_ref, j))
    _cast_rows(cast_in, cast_out)


def _ffn_kernel(seq_len, final, n_cast, *refs):
    (x_ref, p_ref, g_ref, w_up_ref, cw_ref, cb_ref, w_down_ref, pg_ref, w_gate_ref,
     w_ple_ref, fg_ref) = refs[:_N_FFN_IN]
    cast_in = refs[_N_FFN_IN:_N_FFN_IN + n_cast]
    o_ref = refs[_N_FFN_IN + n_cast]
    cast_out = refs[_N_FFN_IN + n_cast + 1:_N_FFN_IN + 2 * n_cast + 1]
    up_ref, act_ref = refs[_N_FFN_IN + 2 * n_cast + 1:]
    tm = x_ref.shape[0]
    t0 = (pl.program_id(0) * tm) & (seq_len - 1)

    @pl.when(t0 == 0)
    def _():
        up_ref[:, 0:CONV_HALO, :] = jnp.zeros((up_ref.shape[0], CONV_HALO, LANES), jnp.float32)

    pb = _bf16(p_ref[...])
    pe = [_dot(pb, _col(w_ple_ref, j)) for j in range(N_SLABS)]
    x = x_ref[...]
    h = _rmsnorm_bf16(x, g_ref)

    def conv(slab):
        up = _dot(h, _col(w_up_ref, slab))
        halves = []
        for half in range(SLAB // LANES):
            s = 2 * slab + half
            cols = slice(s * LANES, (s + 1) * LANES)
            cur = up[:, half * LANES:(half + 1) * LANES]
            up_ref[s, CONV_HALO:CONV_HALO + tm, :] = cur
            out = cb_ref[:, cols]
            for k in range(CONV_WIDTH - 1):
                lo = CONV_HALO - (CONV_WIDTH - 1) + k
                out = out + cw_ref[k:k + 1, cols] * up_ref[s, lo:lo + tm, :]
            halves.append(out + cw_ref[CONV_WIDTH - 1:CONV_WIDTH, cols] * cur)
        return _cat(halves)

    for j in range(FF_SLABS):
        a = conv(j)
        b = conv(FF_SLABS + j)
        act_ref[:, j * SLAB:(j + 1) * SLAB] = _bf16(_gelu(a) * b)
    up_ref[:, 0:CONV_HALO, :] = up_ref[:, tm:tm + CONV_HALO, :]
    act = act_ref[...]
    x = [_col(x_ref, j) + _dot(act, _col(w_down_ref, j)) for j in range(N_SLABS)]

    x_scale = _rms_scale(x)
    hp = _bf16(_cat([x[j] * x_scale * _col(pg_ref, j) for j in range(N_SLABS)]))
    for j in range(N_SLABS):
        x[j] = x[j] + _sigmoid(_dot(hp, _col(w_gate_ref, j))) * pe[j]
    if final:
        x_scale = _rms_scale(x)
        x = [x[j] * x_scale * _col(fg_ref, j) for j in range(N_SLABS)]
    for j in range(N_SLABS):
        o_ref[:, j * SLAB:(j + 1) * SLAB] = x[j]
    _cast_rows(cast_in, cast_out)


def _layer_block(w, layer):
    rest = w.shape[1:]
    return pl.BlockSpec((None,) + rest, lambda i: (layer,) + (0,) * len(rest),
                        pipeline_mode=pl.Buffered(1))


def _whole(w):
    return pl.BlockSpec(w.shape, lambda i: (0,) * w.ndim, pipeline_mode=pl.Buffered(1))


def _padded_width(n):
    return n + (LANES if (n // LANES) % SUBLANES == 0 else 0)


def _cast_plan(w, layer, steps):
    _, k, n = w.shape
    rows = next(r for r in range(BF16_ROWS, k + 1, BF16_ROWS) if k % r == 0 and k // r <= steps)
    last = k // rows - 1
    npad = _padded_width(n)
    return (pl.BlockSpec((None, rows, n), lambda i: (layer, jnp.minimum(i, last), 0)),
            pl.BlockSpec((rows, npad), lambda i: (jnp.minimum(i, last), 0)),
            jax.ShapeDtypeStruct((k, npad), jnp.bfloat16))


def _compiler_params():
    return pltpu.CompilerParams(dimension_semantics=("arbitrary",),
                                vmem_limit_bytes=V7X_VMEM_LIMIT_BYTES)


def _call(body, name, x, tile_inputs, tile_specs, params, param_specs, to_cast, cast_layer,
          scratch):
    n, d = x.shape
    tm = TOKEN_TILE
    steps = n // tm
    tile = pl.BlockSpec((tm, d), lambda i: (i, 0))
    plans = [_cast_plan(w, cast_layer, steps) for w in to_cast]
    out = pl.pallas_call(
        functools.partial(body, len(plans)),
        grid=(steps,),
        in_specs=[tile] + tile_specs + param_specs + [pl_in for pl_in, _, _ in plans],
        out_specs=[tile] + [pl_out for _, pl_out, _ in plans],
        out_shape=[jax.ShapeDtypeStruct((n, d), jnp.float32)] + [s for _, _, s in plans],
        scratch_shapes=scratch,
        compiler_params=_compiler_params(),
        name=name,
    )(x, *tile_inputs, *params, *to_cast)
    return out[0], out[1:]


def _weight(w):
    return _bf16(jnp.pad(w, ((0, 0), (0, _padded_width(w.shape[1]) - w.shape[1]))))


def kernel(x, p, mix_norm, w_in, w_pool, pool_scale, sgu_norm, w_spatial, b_spatial, w_branch_a, w_branch_b, w_out, ffn_norm, w_up, conv_w, conv_b, w_down, ple_norm, w_ple_gate, w_ple, final_norm):
    batch, seq_len, d = x.shape
    depth = w_in.shape[0]
    tm = TOKEN_TILE
    assert d == D_MODEL and seq_len % tm == 0 and seq_len & (seq_len - 1) == 0
    assert tm % SGU_CHUNK == 0
    n = batch * seq_len
    steps = n // tm
    row = lambda v: v.reshape(depth, 1, -1)
    bs = jnp.repeat(jnp.swapaxes(b_spatial, 1, 2), SGU_HEAD_DIM, axis=2)
    w_pool_rows = w_pool.reshape(depth, D_MODEL, POOL_GROUP_WIDTH)
    mixer_f32 = (w_in, w_pool_rows, w_branch_a, w_branch_b, w_out)
    ffn_f32 = (w_up, w_down, w_ple_gate, w_ple)
    mixer_scratch = [pltpu.VMEM((d // LANES, tm + POOL_HALO, LANES), jnp.float32)]
    ffn_scratch = [pltpu.VMEM((2 * D_FF // LANES, tm + CONV_HALO, LANES), jnp.float32),
                   pltpu.VMEM((tm, D_FF), jnp.bfloat16)]
    xf = x.reshape(n, d)
    pf = p.reshape(depth * n, PLE_DIM)
    fg = final_norm.reshape(1, d)
    mixer_bf16 = [_weight(w[0]) for w in mixer_f32]
    for i in range(depth):
        last = i == depth - 1
        wi, wp, wa, wb, wo = mixer_bf16
        wp = wp.reshape(len(POOL_WINDOWS), POOL_GROUP_WIDTH, POOL_GROUP_WIDTH)
        params = (row(mix_norm), wi, wp, row(pool_scale), row(sgu_norm), w_spatial, bs, wa, wb, wo)
        specs = [_whole(q) if q.dtype == jnp.bfloat16 else _layer_block(q, i) for q in params]
        xf, (wu, wd, wg, wl) = _call(
            functools.partial(_mixer_kernel, seq_len), "mixer", xf, (), [], params, specs,
            ffn_f32, i, mixer_scratch)
        params = (row(ffn_norm), wu, conv_w, row(conv_b), wd, row(ple_norm), wg, wl, fg)
        specs = [_whole(q) if q.dtype == jnp.bfloat16 or q is fg else _layer_block(q, i)
                 for q in params]
        p_tile = pl.BlockSpec((tm, PLE_DIM), lambda j, i=i: (i * steps + j, 0))
        xf, mixer_bf16 = _call(
            functools.partial(_ffn_kernel, seq_len, last), "ffn_final" if last else "ffn",
            xf, (pf,), [p_tile], params, specs, () if last else mixer_f32, i + 1, ffn_scratch)
    return xf.reshape(batch, seq_len, d)
```

```python
import functools

import jax
import jax.numpy as jnp
import numpy as np
from jax import lax
from jax.experimental import pallas as pl
from jax.experimental.pallas import tpu as pltpu

LANES = 128
SUBLANES = 8
BF16_ROWS = 16
SLAB = 256
D_MODEL = 1024
N_SLABS = D_MODEL // SLAB
POOL_WINDOWS = (2, 4, 8, 16)
POOL_GROUP_WIDTH = SLAB
POOL_HALO = 16
SGU_CHUNK = 128
SGU_HEADS = 8
SGU_HEAD_DIM = D_MODEL // SGU_HEADS
D_FF = 2816
FF_SLABS = D_FF // SLAB
CONV_WIDTH = 3
CONV_HALO = 8
PLE_DIM = 256
EPS = 1e-6

TOKEN_TILE = 512
FFN_TOKEN_TILE = 1024
CONV_RING = 8
V7X_VMEM_LIMIT_BYTES = 58 * 1024 * 1024

_SQRT_HALF = float(np.sqrt(0.5))
_N_MIXER_IN = 11
_N_FFN_IN = 11


def _gelu(x):
    return (0.5 * x) * (1.0 + lax.erf(x * _SQRT_HALF))


def _sigmoid(x):
    return 1.0 / (1.0 + jnp.exp(-x))


def _dot(a, b):
    return jnp.dot(a, b, preferred_element_type=jnp.float32)


def _bf16(x):
    return x.astype(jnp.bfloat16)


def _cat(slabs):
    return jnp.concatenate(slabs, axis=1)


def _rms_scale(slabs):
    ssq = sum(jnp.sum(s * s, axis=-1, keepdims=True) for s in slabs)
    width = sum(s.shape[1] for s in slabs)
    return lax.rsqrt(ssq * (1.0 / width) + EPS)


def _rmsnorm_bf16(x, g_ref):
    return _bf16(x * _rms_scale([x]) * g_ref[...])


def _col(ref, j):
    return ref[:, j * SLAB:(j + 1) * SLAB]


def _cast_rows(src_refs, dst_refs):
    for src, dst in zip(src_refs, dst_refs):
        rows, n = src.shape
        dst[:, 0:n] = _bf16(src[...])
        if dst.shape[1] > n:
            dst[:, n:] = jnp.zeros((rows, dst.shape[1] - n), jnp.bfloat16)


def _mixer_kernel(seq_len, n_cast, *refs):
    (x_ref, g_ref, w_in_ref, w_pool_ref, pscale_ref, sgu_g_ref, ws_ref, bs_ref,
     wa_ref, wb_ref, wo_ref) = refs[:_N_MIXER_IN]
    cast_in = refs[_N_MIXER_IN:_N_MIXER_IN + n_cast]
    o_ref = refs[_N_MIXER_IN + n_cast]
    cast_out = refs[_N_MIXER_IN + n_cast + 1:_N_MIXER_IN + 2 * n_cast + 1]
    (zp_ref,) = refs[_N_MIXER_IN + 2 * n_cast + 1:]
    tm = x_ref.shape[0]
    t0 = (pl.program_id(0) * tm) & (seq_len - 1)

    @pl.when(t0 == 0)
    def _():
        zp_ref[:, 0:POOL_HALO, :] = jnp.zeros((2 * N_SLABS, POOL_HALO, LANES), jnp.float32)

    x = x_ref[...]
    h = _rmsnorm_bf16(x, g_ref)

    slab_dots = lambda first: [_dot(h, _col(w_in_ref, first + j)) for j in range(N_SLABS)]

    z = slab_dots(0)
    u = slab_dots(N_SLABS)
    v = slab_dots(2 * N_SLABS)
    t_pos = lax.broadcasted_iota(jnp.int32, (tm, LANES), 0) + t0
    y_pool = []
    for gi, w in enumerate(POOL_WINDOWS):
        inv_cnt = 1.0 / jnp.minimum(t_pos + 1, w).astype(jnp.float32)
        pooled = []
        for half in range(SLAB // LANES):
            s = 2 * gi + half
            z_self = z[gi][:, half * LANES:(half + 1) * LANES]
            zp_ref[s, POOL_HALO:POOL_HALO + tm, :] = z_self
            acc = z_self
            for k in range(1, w):
                acc = acc + zp_ref[s, POOL_HALO - k:POOL_HALO - k + tm, :]
            pooled.append(acc * inv_cnt - z_self)
        y_pool.append(_dot(_bf16(_cat(pooled)), w_pool_ref[gi]) * _col(pscale_ref, gi))
    zp_ref[:, 0:POOL_HALO, :] = zp_ref[:, tm:tm + POOL_HALO, :]
    y_pool = _bf16(_cat(y_pool))
    g_a = slab_dots(3 * N_SLABS)
    y_a = [_dot(y_pool, _col(wa_ref, j)) for j in range(N_SLABS)]

    u = [_gelu(s) for s in u]
    v = [_gelu(s) for s in v]
    v_scale = _rms_scale(v)
    vn = [_bf16(v[j] * v_scale * _col(sgu_g_ref, j)) for j in range(N_SLABS)]
    row = lax.broadcasted_iota(jnp.int32, (SGU_CHUNK, SGU_CHUNK), 0)
    col = lax.broadcasted_iota(jnp.int32, (SGU_CHUNK, SGU_CHUNK), 1)
    tril = row >= col
    ws = [_bf16(jnp.where(tril, ws_ref[hh], 0.0)) for hh in range(SGU_HEADS)]
    mixed = []
    for j in range(N_SLABS):
        chunks = []
        for c in range(tm // SGU_CHUNK):
            rows = slice(c * SGU_CHUNK, (c + 1) * SGU_CHUNK)
            heads = []
            for half in range(SLAB // SGU_HEAD_DIM):
                hcols = slice(half * SGU_HEAD_DIM, (half + 1) * SGU_HEAD_DIM)
                heads.append(_dot(ws[2 * j + half], vn[j][rows, hcols]))
            chunks.append(_cat(heads) + _col(bs_ref, j))
        mixed.append(jnp.concatenate(chunks, axis=0))
    g_b = slab_dots(4 * N_SLABS)
    sgu = _bf16(_cat([u[j] * mixed[j] for j in range(N_SLABS)]))
    y_b = [_dot(sgu, _col(wb_ref, j)) for j in range(N_SLABS)]

    merged = _bf16(_cat([_sigmoid(g_a[j]) * y_a[j] + _sigmoid(g_b[j]) * y_b[j]
                         for j in range(N_SLABS)]))
    for j in range(N_SLABS):
        o_ref[:, j * SLAB:(j + 1) * SLAB] = _col(x_ref, j) + _dot(merged, _col(wo_ref, j))
    _cast_rows(cast_in, cast_out)


def _ffn_kernel(seq_len, final, n_cast, *refs):
    (x_ref, p_ref, g_ref, w_up_ref, cw_ref, cb_ref, w_down_ref, pg_ref, w_gate_ref,
     w_ple_ref, fg_ref) = refs[:_N_FFN_IN]
    cast_in = refs[_N_FFN_IN:_N_FFN_IN + n_cast]
    o_ref = refs[_N_FFN_IN + n_cast]
    cast_out = refs[_N_FFN_IN + n_cast + 1:_N_FFN_IN + 2 * n_cast + 1]
    up_ref, carry_ref, act_ref = refs[_N_FFN_IN + 2 * n_cast + 1:]
    tm = x_ref.shape[0]
    t0 = (pl.program_id(0) * tm) & (seq_len - 1)

    @pl.when(t0 == 0)
    def _():
        carry_ref[...] = jnp.zeros(carry_ref.shape, jnp.float32)

    x = x_ref[...]
    h = _rmsnorm_bf16(x, g_ref)

    def conv(slab):
        up = _dot(h, _col(w_up_ref, slab))
        halves = []
        for half in range(SLAB // LANES):
            s = 2 * slab + half
            cols = slice(s * LANES, (s + 1) * LANES)
            cur = up[:, half * LANES:(half + 1) * LANES]
            r = s % CONV_RING
            up_ref[r, 0:CONV_HALO, :] = carry_ref[s]
            up_ref[r, CONV_HALO:CONV_HALO + tm, :] = cur
            carry_ref[s] = cur[tm - CONV_HALO:, :]
            out = cb_ref[:, cols]
            for k in range(CONV_WIDTH - 1):
                lo = CONV_HALO - (CONV_WIDTH - 1) + k
                out = out + cw_ref[k:k + 1, cols] * up_ref[r, lo:lo + tm, :]
            halves.append(out + cw_ref[CONV_WIDTH - 1:CONV_WIDTH, cols] * cur)
        return _cat(halves)

    for j in range(FF_SLABS):
        a = conv(j)
        b = conv(FF_SLABS + j)
        act_ref[:, j * SLAB:(j + 1) * SLAB] = _bf16(_gelu(a) * b)
    act = act_ref[...]
    x = [_col(x_ref, j) + _dot(act, _col(w_down_ref, j)) for j in range(N_SLABS)]

    x_scale = _rms_scale(x)
    hp = _bf16(_cat([x[j] * x_scale * _col(pg_ref, j) for j in range(N_SLABS)]))
    pb = _bf16(p_ref[...])
    for j in range(N_SLABS):
        gate = _sigmoid(_dot(hp, _col(w_gate_ref, j)))
        x[j] = x[j] + gate * _dot(pb, _col(w_ple_ref, j))
    if final:
        x_scale = _rms_scale(x)
        x = [x[j] * x_scale * _col(fg_ref, j) for j in range(N_SLABS)]
    for j in range(N_SLABS):
        o_ref[:, j * SLAB:(j + 1) * SLAB] = x[j]
    _cast_rows(cast_in, cast_out)


def _layer_block(w, layer):
    rest = w.shape[1:]
    return pl.BlockSpec((None,) + rest, lambda i: (layer,) + (0,) * len(rest),
                        pipeline_mode=pl.Buffered(1))


def _whole(w):
    return pl.BlockSpec(w.shape, lambda i: (0,) * w.ndim, pipeline_mode=pl.Buffered(1))


def _padded_width(n):
    return n + (LANES if (n // LANES) % SUBLANES == 0 else 0)


def _cast_plan(w, layer, steps):
    _, k, n = w.shape
    rows = next(r for r in range(BF16_ROWS, k + 1, BF16_ROWS) if k % r == 0 and k // r <= steps)
    last = k // rows - 1
    npad = _padded_width(n)
    return (pl.BlockSpec((None, rows, n), lambda i: (layer, jnp.minimum(i, last), 0)),
            pl.BlockSpec((rows, npad), lambda i: (jnp.minimum(i, last), 0)),
            jax.ShapeDtypeStruct((k, npad), jnp.bfloat16))


def _compiler_params():
    return pltpu.CompilerParams(dimension_semantics=("arbitrary",),
                                vmem_limit_bytes=V7X_VMEM_LIMIT_BYTES)


def _call(body, name, tm, x, tile_inputs, tile_specs, params, param_specs, to_cast, cast_layer,
          scratch):
    n, d = x.shape
    steps = n // tm
    tile = pl.BlockSpec((tm, d), lambda i: (i, 0))
    plans = [_cast_plan(w, cast_layer, steps) for w in to_cast]
    out = pl.pallas_call(
        functools.partial(body, len(plans)),
        grid=(steps,),
        in_specs=[tile] + tile_specs + param_specs + [pl_in for pl_in, _, _ in plans],
        out_specs=[tile] + [pl_out for _, pl_out, _ in plans],
        out_shape=[jax.ShapeDtypeStruct((n, d), jnp.float32)] + [s for _, _, s in plans],
        scratch_shapes=scratch,
        compiler_params=_compiler_params(),
        name=name,
    )(x, *tile_inputs, *params, *to_cast)
    return out[0], out[1:]


def _weight(w):
    return _bf16(jnp.pad(w, ((0, 0), (0, _padded_width(w.shape[1]) - w.shape[1]))))


def kernel(x, p, mix_norm, w_in, w_pool, pool_scale, sgu_norm, w_spatial, b_spatial, w_branch_a, w_branch_b, w_out, ffn_norm, w_up, conv_w, conv_b, w_down, ple_norm, w_ple_gate, w_ple, final_norm):
    batch, seq_len, d = x.shape
    depth = w_in.shape[0]
    tm, tf = TOKEN_TILE, FFN_TOKEN_TILE
    assert d == D_MODEL and seq_len % tm == 0 and seq_len % tf == 0
    assert seq_len & (seq_len - 1) == 0 and tm % SGU_CHUNK == 0
    n = batch * seq_len
    row = lambda v: v.reshape(depth, 1, -1)
    bs = jnp.repeat(jnp.swapaxes(b_spatial, 1, 2), SGU_HEAD_DIM, axis=2)
    w_pool_rows = w_pool.reshape(depth, D_MODEL, POOL_GROUP_WIDTH)
    mixer_f32 = (w_in, w_pool_rows, w_branch_a, w_branch_b, w_out)
    ffn_f32 = (w_up, w_down, w_ple_gate, w_ple)
    mixer_scratch = [pltpu.VMEM((d // LANES, tm + POOL_HALO, LANES), jnp.float32)]
    ffn_scratch = [pltpu.VMEM((CONV_RING, tf + CONV_HALO, LANES), jnp.float32),
                   pltpu.VMEM((2 * D_FF // LANES, CONV_HALO, LANES), jnp.float32),
                   pltpu.VMEM((tf, D_FF), jnp.bfloat16)]
    xf = x.reshape(n, d)
    pf = p.reshape(depth * n, PLE_DIM)
    fg = final_norm.reshape(1, d)
    mixer_bf16 = [_weight(w[0]) for w in mixer_f32]
    for i in range(depth):
        last = i == depth - 1
        wi, wp, wa, wb, wo = mixer_bf16
        wp = wp.reshape(len(POOL_WINDOWS), POOL_GROUP_WIDTH, POOL_GROUP_WIDTH)
        params = (row(mix_norm), wi, wp, row(pool_scale), row(sgu_norm), w_spatial, bs, wa, wb, wo)
        specs = [_whole(q) if q.dtype == jnp.bfloat16 else _layer_block(q, i) for q in params]
        xf, (wu, wd, wg, wl) = _call(
            functools.partial(_mixer_kernel, seq_len), "mixer", tm, xf, (), [], params, specs,
            ffn_f32, i, mixer_scratch)
        params = (row(ffn_norm), wu, conv_w, row(conv_b), wd, row(ple_norm), wg, wl, fg)
        specs = [_whole(q) if q.dtype == jnp.bfloat16 or q is fg else _layer_block(q, i)
                 for q in params]
        p_tile = pl.BlockSpec((tf, PLE_DIM), lambda j, i=i: (i * (n // tf) + j, 0))
        xf, mixer_bf16 = _call(
            functools.partial(_ffn_kernel, seq_len, last), "ffn_final" if last else "ffn",
            tf, xf, (pf,), [p_tile], params, specs, () if last else mixer_f32, i + 1, ffn_scratch)
    return xf.reshape(batch, seq_len, d)
```

```python
import functools

import jax
import jax.numpy as jnp
import numpy as np
from jax import lax
from jax.experimental import pallas as pl
from jax.experimental.pallas import tpu as pltpu

LANES = 128
SUBLANES = 8
BF16_ROWS = 16
SLAB = 256
D_MODEL = 1024
N_SLABS = D_MODEL // SLAB
POOL_WINDOWS = (2, 4, 8, 16)
POOL_GROUP_WIDTH = SLAB
POOL_HALO = 16
SGU_CHUNK = 128
SGU_HEADS = 8
SGU_HEAD_DIM = D_MODEL // SGU_HEADS
D_FF = 2816
FF_SLABS = D_FF // SLAB
CONV_WIDTH = 3
CONV_HALO = 8
PLE_DIM = 256
EPS = 1e-6

TOKEN_TILE = 512
FFN_TOKEN_TILE = 1024
CONV_RING = 8
V7X_VMEM_LIMIT_BYTES = 58 * 1024 * 1024

_SQRT_HALF = float(np.sqrt(0.5))
_N_MIXER_IN = 11
_N_FFN_IN = 11


def _gelu(x):
    return (0.5 * x) * (1.0 + lax.erf(x * _SQRT_HALF))


def _sigmoid(x):
    return 0.5 * jnp.tanh(0.5 * x) + 0.5


def _dot(a, b):
    return jnp.dot(a, b, preferred_element_type=jnp.float32)


def _bf16(x):
    return x.astype(jnp.bfloat16)


def _cat(slabs):
    return jnp.concatenate(slabs, axis=1)


def _rms_scale(slabs):
    ssq = sum(jnp.sum(s * s, axis=-1, keepdims=True) for s in slabs)
    width = sum(s.shape[1] for s in slabs)
    return lax.rsqrt(ssq * (1.0 / width) + EPS)


def _rmsnorm_bf16(x, g_ref):
    return _bf16(x * _rms_scale([x]) * g_ref[...])


def _col(ref, j):
    return ref[:, j * SLAB:(j + 1) * SLAB]


def _layer_row(ref, layer):
    return ref.at[pl.ds(layer, 1)]


def _cast_rows(src_refs, dst_refs):
    for src, dst in zip(src_refs, dst_refs):
        rows, n = src.shape
        dst[:, 0:n] = _bf16(src[...])
        if dst.shape[1] > n:
            dst[:, n:] = jnp.zeros((rows, dst.shape[1] - n), jnp.bfloat16)


def _mixer_kernel(seq_len, layer, n_cast, *refs):
    (x_ref, g_ref, w_in_ref, w_pool_ref, pscale_ref, sgu_g_ref, ws_ref, bs_ref,
     wa_ref, wb_ref, wo_ref) = refs[:_N_MIXER_IN]
    cast_in = refs[_N_MIXER_IN:_N_MIXER_IN + n_cast]
    o_ref = refs[_N_MIXER_IN + n_cast]
    cast_out = refs[_N_MIXER_IN + n_cast + 1:_N_MIXER_IN + 2 * n_cast + 1]
    (zp_ref,) = refs[_N_MIXER_IN + 2 * n_cast + 1:]
    g_ref, pscale_ref, sgu_g_ref = (_layer_row(r, layer) for r in (g_ref, pscale_ref, sgu_g_ref))
    tm = x_ref.shape[0]
    t0 = (pl.program_id(0) * tm) & (seq_len - 1)

    @pl.when(t0 == 0)
    def _():
        zp_ref[:, 0:POOL_HALO, :] = jnp.zeros((2 * N_SLABS, POOL_HALO, LANES), jnp.float32)

    x = x_ref[...]
    h = _rmsnorm_bf16(x, g_ref)

    slab_dots = lambda first: [_dot(h, _col(w_in_ref, first + j)) for j in range(N_SLABS)]

    z = slab_dots(0)
    u = slab_dots(N_SLABS)
    v = slab_dots(2 * N_SLABS)
    t_pos = lax.broadcasted_iota(jnp.int32, (tm, LANES), 0) + t0
    y_pool = []
    for gi, w in enumerate(POOL_WINDOWS):
        inv_cnt = 1.0 / jnp.minimum(t_pos + 1, w).astype(jnp.float32)
        pooled = []
        for half in range(SLAB // LANES):
            s = 2 * gi + half
            z_self = z[gi][:, half * LANES:(half + 1) * LANES]
            zp_ref[s, POOL_HALO:POOL_HALO + tm, :] = z_self
            acc = z_self
            for k in range(1, w):
                acc = acc + zp_ref[s, POOL_HALO - k:POOL_HALO - k + tm, :]
            pooled.append(acc * inv_cnt - z_self)
        y_pool.append(_dot(_bf16(_cat(pooled)), w_pool_ref[gi]) * _col(pscale_ref, gi))
    zp_ref[:, 0:POOL_HALO, :] = zp_ref[:, tm:tm + POOL_HALO, :]
    y_pool = _bf16(_cat(y_pool))
    g_a = slab_dots(3 * N_SLABS)
    y_a = [_dot(y_pool, _col(wa_ref, j)) for j in range(N_SLABS)]

    u = [_gelu(s) for s in u]
    v = [_gelu(s) for s in v]
    v_scale = _rms_scale(v)
    vn = [_bf16(v[j] * v_scale * _col(sgu_g_ref, j)) for j in range(N_SLABS)]
    row = lax.broadcasted_iota(jnp.int32, (SGU_CHUNK, SGU_CHUNK), 0)
    col = lax.broadcasted_iota(jnp.int32, (SGU_CHUNK, SGU_CHUNK), 1)
    tril = row >= col
    ws = [_bf16(jnp.where(tril, ws_ref[hh], 0.0)) for hh in range(SGU_HEADS)]
    mixed = []
    for j in range(N_SLABS):
        chunks = []
        for c in range(tm // SGU_CHUNK):
            rows = slice(c * SGU_CHUNK, (c + 1) * SGU_CHUNK)
            heads = []
            for half in range(SLAB // SGU_HEAD_DIM):
                hcols = slice(half * SGU_HEAD_DIM, (half + 1) * SGU_HEAD_DIM)
                heads.append(_dot(ws[2 * j + half], vn[j][rows, hcols]))
            chunks.append(_cat(heads) + _col(bs_ref, j))
        mixed.append(jnp.concatenate(chunks, axis=0))
    g_b = slab_dots(4 * N_SLABS)
    sgu = _bf16(_cat([u[j] * mixed[j] for j in range(N_SLABS)]))
    y_b = [_dot(sgu, _col(wb_ref, j)) for j in range(N_SLABS)]

    merged = _bf16(_cat([_sigmoid(g_a[j]) * y_a[j] + _sigmoid(g_b[j]) * y_b[j]
                         for j in range(N_SLABS)]))
    for j in range(N_SLABS):
        o_ref[:, j * SLAB:(j + 1) * SLAB] = _col(x_ref, j) + _dot(merged, _col(wo_ref, j))
    _cast_rows(cast_in, cast_out)


def _ffn_kernel(seq_len, layer, final, n_cast, *refs):
    (x_ref, p_ref, g_ref, w_up_ref, cw_ref, cb_ref, w_down_ref, pg_ref, w_gate_ref,
     w_ple_ref, fg_ref) = refs[:_N_FFN_IN]
    cast_in = refs[_N_FFN_IN:_N_FFN_IN + n_cast]
    o_ref = refs[_N_FFN_IN + n_cast]
    cast_out = refs[_N_FFN_IN + n_cast + 1:_N_FFN_IN + 2 * n_cast + 1]
    up_ref, carry_ref, act_ref = refs[_N_FFN_IN + 2 * n_cast + 1:]
    g_ref, cb_ref, pg_ref = (_layer_row(r, layer) for r in (g_ref, cb_ref, pg_ref))
    tm = x_ref.shape[0]
    t0 = (pl.program_id(0) * tm) & (seq_len - 1)

    @pl.when(t0 == 0)
    def _():
        carry_ref[...] = jnp.zeros(carry_ref.shape, jnp.float32)

    x = x_ref[...]
    h = _rmsnorm_bf16(x, g_ref)

    def conv(slab):
        up = _dot(h, _col(w_up_ref, slab))
        halves = []
        for half in range(SLAB // LANES):
            s = 2 * slab + half
            cols = slice(s * LANES, (s + 1) * LANES)
            cur = up[:, half * LANES:(half + 1) * LANES]
            r = s % CONV_RING
            up_ref[r, 0:CONV_HALO, :] = carry_ref[s]
            up_ref[r, CONV_HALO:CONV_HALO + tm, :] = cur
            carry_ref[s] = cur[tm - CONV_HALO:, :]
            out = cb_ref[:, cols]
            for k in range(CONV_WIDTH - 1):
                lo = CONV_HALO - (CONV_WIDTH - 1) + k
                out = out + cw_ref[k:k + 1, cols] * up_ref[r, lo:lo + tm, :]
            halves.append(out + cw_ref[CONV_WIDTH - 1:CONV_WIDTH, cols] * cur)
        return _cat(halves)

    for j in range(FF_SLABS):
        a = conv(j)
        b = conv(FF_SLABS + j)
        act_ref[:, j * SLAB:(j + 1) * SLAB] = _bf16(_gelu(a) * b)
    act = act_ref[...]
    x = [_col(x_ref, j) + _dot(act, _col(w_down_ref, j)) for j in range(N_SLABS)]

    x_scale = _rms_scale(x)
    hp = _bf16(_cat([x[j] * x_scale * _col(pg_ref, j) for j in range(N_SLABS)]))
    pb = _bf16(p_ref[...])
    for j in range(N_SLABS):
        gate = _sigmoid(_dot(hp, _col(w_gate_ref, j)))
        x[j] = x[j] + gate * _dot(pb, _col(w_ple_ref, j))
    if final:
        x_scale = _rms_scale(x)
        x = [x[j] * x_scale * _col(fg_ref, j) for j in range(N_SLABS)]
    for j in range(N_SLABS):
        o_ref[:, j * SLAB:(j + 1) * SLAB] = x[j]
    _cast_rows(cast_in, cast_out)


def _layer_block(w, layer):
    rest = w.shape[1:]
    return pl.BlockSpec((None,) + rest, lambda i: (layer,) + (0,) * len(rest),
                        pipeline_mode=pl.Buffered(1))


def _whole(w):
    return pl.BlockSpec(w.shape, lambda i: (0,) * w.ndim, pipeline_mode=pl.Buffered(1))


def _padded_width(n):
    return n + (LANES if (n // LANES) % SUBLANES == 0 else 0)


def _cast_plan(w, layer, steps):
    _, k, n = w.shape
    rows = next(r for r in range(BF16_ROWS, k + 1, BF16_ROWS) if k % r == 0 and k // r <= steps)
    last = k // rows - 1
    npad = _padded_width(n)
    return (pl.BlockSpec((None, rows, n), lambda i: (layer, jnp.minimum(i, last), 0)),
            pl.BlockSpec((rows, npad), lambda i: (jnp.minimum(i, last), 0)),
            jax.ShapeDtypeStruct((k, npad), jnp.bfloat16))


def _compiler_params():
    return pltpu.CompilerParams(dimension_semantics=("arbitrary",),
                                vmem_limit_bytes=V7X_VMEM_LIMIT_BYTES)


def _call(body, name, tm, x, tile_inputs, tile_specs, params, param_specs, to_cast, cast_layer,
          scratch):
    n, d = x.shape
    steps = n // tm
    tile = pl.BlockSpec((tm, d), lambda i: (i, 0))
    plans = [_cast_plan(w, cast_layer, steps) for w in to_cast]
    out = pl.pallas_call(
        functools.partial(body, len(plans)),
        grid=(steps,),
        in_specs=[tile] + tile_specs + param_specs + [pl_in for pl_in, _, _ in plans],
        out_specs=[tile] + [pl_out for _, pl_out, _ in plans],
        out_shape=[jax.ShapeDtypeStruct((n, d), jnp.float32)] + [s for _, _, s in plans],
        scratch_shapes=scratch,
        compiler_params=_compiler_params(),
        name=name,
    )(x, *tile_inputs, *params, *to_cast)
    return out[0], out[1:]


def _weight(w):
    k, n = w.shape
    return jnp.zeros((k, _padded_width(n)), jnp.bfloat16).at[:, :n].set(_bf16(w))


def kernel(x, p, mix_norm, w_in, w_pool, pool_scale, sgu_norm, w_spatial, b_spatial, w_branch_a, w_branch_b, w_out, ffn_norm, w_up, conv_w, conv_b, w_down, ple_norm, w_ple_gate, w_ple, final_norm):
    batch, seq_len, d = x.shape
    depth = w_in.shape[0]
    tm, tf = TOKEN_TILE, FFN_TOKEN_TILE
    assert d == D_MODEL and seq_len % tm == 0 and seq_len % tf == 0
    assert seq_len & (seq_len - 1) == 0 and tm % SGU_CHUNK == 0
    n = batch * seq_len
    bs = jnp.repeat(jnp.swapaxes(b_spatial, 1, 2), SGU_HEAD_DIM, axis=2)
    w_pool_rows = w_pool.reshape(depth, D_MODEL, POOL_GROUP_WIDTH)
    mixer_f32 = (w_in, w_pool_rows, w_branch_a, w_branch_b, w_out)
    ffn_f32 = (w_up, w_down, w_ple_gate, w_ple)
    mixer_scratch = [pltpu.VMEM((d // LANES, tm + POOL_HALO, LANES), jnp.float32)]
    ffn_scratch = [pltpu.VMEM((CONV_RING, tf + CONV_HALO, LANES), jnp.float32),
                   pltpu.VMEM((2 * D_FF // LANES, CONV_HALO, LANES), jnp.float32),
                   pltpu.VMEM((tf, D_FF), jnp.bfloat16)]
    xf = x.reshape(n, d)
    pf = p.reshape(depth * n, PLE_DIM)
    fg = final_norm.reshape(1, d)
    mixer_bf16 = [_weight(w[0]) for w in mixer_f32]
    for i in range(depth):
        last = i == depth - 1
        wi, wp, wa, wb, wo = mixer_bf16
        wp = wp.reshape(len(POOL_WINDOWS), POOL_GROUP_WIDTH, POOL_GROUP_WIDTH)
        params = (mix_norm, wi, wp, pool_scale, sgu_norm, w_spatial, bs, wa, wb, wo)
        specs = [_layer_block(q, i) if q.ndim > 2 and q.dtype == jnp.float32 else _whole(q)
                 for q in params]
        xf, (wu, wd, wg, wl) = _call(
            functools.partial(_mixer_kernel, seq_len, i), "mixer", tm, xf, (), [], params, specs,
            ffn_f32, i, mixer_scratch)
        params = (ffn_norm, wu, conv_w, conv_b, wd, ple_norm, wg, wl, fg)
        specs = [_layer_block(q, i) if q.ndim > 2 and q.dtype == jnp.float32 else _whole(q)
                 for q in params]
        p_tile = pl.BlockSpec((tf, PLE_DIM), lambda j, i=i: (i * (n // tf) + j, 0))
        xf, mixer_bf16 = _call(
            functools.partial(_ffn_kernel, seq_len, i, last), "ffn_final" if last else "ffn",
            tf, xf, (pf,), [p_tile], params, specs, () if last else mixer_f32, i + 1, ffn_scratch)
    return xf.reshape(batch, seq_len, d)
```

```python
import functools

import jax
import jax.numpy as jnp
import numpy as np
from jax import lax
from jax.experimental import pallas as pl
from jax.experimental.pallas import tpu as pltpu

LANES = 128
SUBLANES = 8
BF16_ROWS = 16
SLAB = 256
D_MODEL = 1024
N_SLABS = D_MODEL // SLAB
POOL_WINDOWS = (2, 4, 8, 16)
POOL_GROUP_WIDTH = SLAB
POOL_HALO = 16
SGU_CHUNK = 128
SGU_HEADS = 8
SGU_HEAD_DIM = D_MODEL // SGU_HEADS
D_FF = 2816
FF_SLABS = D_FF // SLAB
CONV_WIDTH = 3
CONV_HALO = 8
PLE_DIM = 256
EPS = 1e-6

TOKEN_TILE = 1024
MIXER_SUB_TILE = 512
FFN_TOKEN_TILE = 1024
CONV_RING = 8
V7X_VMEM_LIMIT_BYTES = 58 * 1024 * 1024

_SQRT_HALF = float(np.sqrt(0.5))
_N_MIXER_IN = 11
_N_FFN_IN = 11


def _gelu(x):
    return (0.5 * x) * (1.0 + lax.erf(x * _SQRT_HALF))


def _sigmoid(x):
    return 0.5 * jnp.tanh(0.5 * x) + 0.5


def _dot(a, b):
    return jnp.dot(a, b, preferred_element_type=jnp.float32)


def _bf16(x):
    return x.astype(jnp.bfloat16)


def _cat(slabs):
    return jnp.concatenate(slabs, axis=1)


def _rms_scale(slabs):
    ssq = sum(jnp.sum(s * s, axis=-1, keepdims=True) for s in slabs)
    width = sum(s.shape[1] for s in slabs)
    return lax.rsqrt(ssq * (1.0 / width) + EPS)


def _rmsnorm_bf16(x, g_ref):
    return _bf16(x * _rms_scale([x]) * g_ref[...])


def _col(ref, j):
    return ref[:, j * SLAB:(j + 1) * SLAB]


def _layer_row(ref, layer):
    return ref.at[pl.ds(layer, 1)]


def _cast_rows(src_refs, dst_refs):
    for src, dst in zip(src_refs, dst_refs):
        rows, n = src.shape
        dst[:, 0:n] = _bf16(src[...])
        if dst.shape[1] > n:
            dst[:, n:] = jnp.zeros((rows, dst.shape[1] - n), jnp.bfloat16)


def _mixer_kernel(seq_len, layer, n_cast, *refs):
    (x_ref, g_ref, w_in_ref, w_pool_ref, pscale_ref, sgu_g_ref, ws_ref, bs_ref,
     wa_ref, wb_ref, wo_ref) = refs[:_N_MIXER_IN]
    cast_in = refs[_N_MIXER_IN:_N_MIXER_IN + n_cast]
    o_ref = refs[_N_MIXER_IN + n_cast]
    cast_out = refs[_N_MIXER_IN + n_cast + 1:_N_MIXER_IN + 2 * n_cast + 1]
    (zp_ref,) = refs[_N_MIXER_IN + 2 * n_cast + 1:]
    g_ref, pscale_ref, sgu_g_ref = (_layer_row(r, layer) for r in (g_ref, pscale_ref, sgu_g_ref))
    tm = x_ref.shape[0]
    t0 = (pl.program_id(0) * tm) & (seq_len - 1)

    @pl.when(t0 == 0)
    def _():
        zp_ref[:, 0:POOL_HALO, :] = jnp.zeros((2 * N_SLABS, POOL_HALO, LANES), jnp.float32)

    row = lax.broadcasted_iota(jnp.int32, (SGU_CHUNK, SGU_CHUNK), 0)
    col = lax.broadcasted_iota(jnp.int32, (SGU_CHUNK, SGU_CHUNK), 1)
    tril = row >= col
    ws = [_bf16(jnp.where(tril, ws_ref[hh], 0.0)) for hh in range(SGU_HEADS)]
    for r0 in range(0, tm, MIXER_SUB_TILE):
        _mixer_rows(r0, t0 + r0, x_ref, g_ref, w_in_ref, w_pool_ref, pscale_ref, sgu_g_ref, ws,
                    bs_ref, wa_ref, wb_ref, wo_ref, o_ref, zp_ref)
    zp_ref[:, 0:POOL_HALO, :] = zp_ref[:, tm:tm + POOL_HALO, :]
    _cast_rows(cast_in, cast_out)


def _mixer_rows(r0, t_first, x_ref, g_ref, w_in_ref, w_pool_ref, pscale_ref, sgu_g_ref, ws,
                bs_ref, wa_ref, wb_ref, wo_ref, o_ref, zp_ref):
    n = MIXER_SUB_TILE
    x = x_ref[r0:r0 + n, :]
    h = _rmsnorm_bf16(x, g_ref)

    slab_dots = lambda first: [_dot(h, _col(w_in_ref, first + j)) for j in range(N_SLABS)]

    z = slab_dots(0)
    u = slab_dots(N_SLABS)
    v = slab_dots(2 * N_SLABS)
    t_pos = lax.broadcasted_iota(jnp.int32, (n, LANES), 0) + t_first
    base = POOL_HALO + r0
    y_pool = []
    for gi, w in enumerate(POOL_WINDOWS):
        inv_cnt = 1.0 / jnp.minimum(t_pos + 1, w).astype(jnp.float32)
        pooled = []
        for half in range(SLAB // LANES):
            s = 2 * gi + half
            z_self = z[gi][:, half * LANES:(half + 1) * LANES]
            zp_ref[s, base:base + n, :] = z_self
            acc = z_self
            for k in range(1, w):
                acc = acc + zp_ref[s, base - k:base - k + n, :]
            pooled.append(acc * inv_cnt - z_self)
        y_pool.append(_dot(_bf16(_cat(pooled)), w_pool_ref[gi]) * _col(pscale_ref, gi))
    y_pool = _bf16(_cat(y_pool))
    g_a = slab_dots(3 * N_SLABS)
    y_a = [_dot(y_pool, _col(wa_ref, j)) for j in range(N_SLABS)]

    u = [_gelu(s) for s in u]
    v = [_gelu(s) for s in v]
    v_scale = _rms_scale(v)
    vn = [_bf16(v[j] * v_scale * _col(sgu_g_ref, j)) for j in range(N_SLABS)]
    mixed = []
    for j in range(N_SLABS):
        chunks = []
        for c in range(n // SGU_CHUNK):
            rows = slice(c * SGU_CHUNK, (c + 1) * SGU_CHUNK)
            heads = []
            for half in range(SLAB // SGU_HEAD_DIM):
                hcols = slice(half * SGU_HEAD_DIM, (half + 1) * SGU_HEAD_DIM)
                heads.append(_dot(ws[2 * j + half], vn[j][rows, hcols]))
            chunks.append(_cat(heads) + _col(bs_ref, j))
        mixed.append(jnp.concatenate(chunks, axis=0))
    g_b = slab_dots(4 * N_SLABS)
    sgu = _bf16(_cat([u[j] * mixed[j] for j in range(N_SLABS)]))
    y_b = [_dot(sgu, _col(wb_ref, j)) for j in range(N_SLABS)]

    merged = _bf16(_cat([_sigmoid(g_a[j]) * y_a[j] + _sigmoid(g_b[j]) * y_b[j]
                         for j in range(N_SLABS)]))
    for j in range(N_SLABS):
        cols = slice(j * SLAB, (j + 1) * SLAB)
        o_ref[r0:r0 + n, cols] = x_ref[r0:r0 + n, cols] + _dot(merged, _col(wo_ref, j))


def _ffn_kernel(seq_len, layer, final, n_cast, *refs):
    (x_ref, p_ref, g_ref, w_up_ref, cw_ref, cb_ref, w_down_ref, pg_ref, w_gate_ref,
     w_ple_ref, fg_ref) = refs[:_N_FFN_IN]
    cast_in = refs[_N_FFN_IN:_N_FFN_IN + n_cast]
    o_ref = refs[_N_FFN_IN + n_cast]
    cast_out = refs[_N_FFN_IN + n_cast + 1:_N_FFN_IN + 2 * n_cast + 1]
    up_ref, carry_ref, act_ref = refs[_N_FFN_IN + 2 * n_cast + 1:]
    g_ref, cb_ref, pg_ref = (_layer_row(r, layer) for r in (g_ref, cb_ref, pg_ref))
    tm = x_ref.shape[0]
    t0 = (pl.program_id(0) * tm) & (seq_len - 1)

    @pl.when(t0 == 0)
    def _():
        carry_ref[...] = jnp.zeros(carry_ref.shape, jnp.float32)

    x = x_ref[...]
    h = _rmsnorm_bf16(x, g_ref)

    def conv(slab):
        up = _dot(h, _col(w_up_ref, slab))
        halves = []
        for half in range(SLAB // LANES):
            s = 2 * slab + half
            cols = slice(s * LANES, (s + 1) * LANES)
            cur = up[:, half * LANES:(half + 1) * LANES]
            r = s % CONV_RING
            up_ref[r, 0:CONV_HALO, :] = carry_ref[s]
            up_ref[r, CONV_HALO:CONV_HALO + tm, :] = cur
            carry_ref[s] = cur[tm - CONV_HALO:, :]
            out = cb_ref[:, cols]
            for k in range(CONV_WIDTH - 1):
                lo = CONV_HALO - (CONV_WIDTH - 1) + k
                out = out + cw_ref[k:k + 1, cols] * up_ref[r, lo:lo + tm, :]
            halves.append(out + cw_ref[CONV_WIDTH - 1:CONV_WIDTH, cols] * cur)
        return _cat(halves)

    for j in range(FF_SLABS):
        a = conv(j)
        b = conv(FF_SLABS + j)
        act_ref[:, j * SLAB:(j + 1) * SLAB] = _bf16(_gelu(a) * b)
    act = act_ref[...]
    x = [_col(x_ref, j) + _dot(act, _col(w_down_ref, j)) for j in range(N_SLABS)]

    x_scale = _rms_scale(x)
    hp = _bf16(_cat([x[j] * x_scale * _col(pg_ref, j) for j in range(N_SLABS)]))
    pb = _bf16(p_ref[...])
    for j in range(N_SLABS):
        gate = _sigmoid(_dot(hp, _col(w_gate_ref, j)))
        x[j] = x[j] + gate * _dot(pb, _col(w_ple_ref, j))
    if final:
        x_scale = _rms_scale(x)
        x = [x[j] * x_scale * _col(fg_ref, j) for j in range(N_SLABS)]
    for j in range(N_SLABS):
        o_ref[:, j * SLAB:(j + 1) * SLAB] = x[j]
    _cast_rows(cast_in, cast_out)


def _layer_block(w, layer):
    rest = w.shape[1:]
    return pl.BlockSpec((None,) + rest, lambda i: (layer,) + (0,) * len(rest),
                        pipeline_mode=pl.Buffered(1))


def _whole(w):
    return pl.BlockSpec(w.shape, lambda i: (0,) * w.ndim, pipeline_mode=pl.Buffered(1))


def _padded_width(n):
    return n + (LANES if (n // LANES) % SUBLANES == 0 else 0)


def _cast_plan(w, layer, steps):
    _, k, n = w.shape
    rows = next(r for r in range(BF16_ROWS, k + 1, BF16_ROWS) if k % r == 0 and k // r <= steps)
    last = k // rows - 1
    npad = _padded_width(n)
    return (pl.BlockSpec((None, rows, n), lambda i: (layer, jnp.minimum(i, last), 0)),
            pl.BlockSpec((rows, npad), lambda i: (jnp.minimum(i, last), 0)),
            jax.ShapeDtypeStruct((k, npad), jnp.bfloat16))


def _compiler_params():
    return pltpu.CompilerParams(dimension_semantics=("arbitrary",),
                                vmem_limit_bytes=V7X_VMEM_LIMIT_BYTES)


def _call(body, name, tm, x, tile_inputs, tile_specs, params, param_specs, to_cast, cast_layer,
          scratch):
    n, d = x.shape
    steps = n // tm
    tile = pl.BlockSpec((tm, d), lambda i: (i, 0))
    plans = [_cast_plan(w, cast_layer, steps) for w in to_cast]
    out = pl.pallas_call(
        functools.partial(body, len(plans)),
        grid=(steps,),
        in_specs=[tile] + tile_specs + param_specs + [pl_in for pl_in, _, _ in plans],
        out_specs=[tile] + [pl_out for _, pl_out, _ in plans],
        out_shape=[jax.ShapeDtypeStruct((n, d), jnp.float32)] + [s for _, _, s in plans],
        scratch_shapes=scratch,
        compiler_params=_compiler_params(),
        name=name,
    )(x, *tile_inputs, *params, *to_cast)
    return out[0], out[1:]


def _weight(w):
    k, n = w.shape
    return jnp.zeros((k, _padded_width(n)), jnp.bfloat16).at[:, :n].set(_bf16(w))


def kernel(x, p, mix_norm, w_in, w_pool, pool_scale, sgu_norm, w_spatial, b_spatial, w_branch_a, w_branch_b, w_out, ffn_norm, w_up, conv_w, conv_b, w_down, ple_norm, w_ple_gate, w_ple, final_norm):
    batch, seq_len, d = x.shape
    depth = w_in.shape[0]
    tm, tf = TOKEN_TILE, FFN_TOKEN_TILE
    assert d == D_MODEL and seq_len % tm == 0 and seq_len % tf == 0
    assert seq_len & (seq_len - 1) == 0 and tm % MIXER_SUB_TILE == 0
    assert MIXER_SUB_TILE % SGU_CHUNK == 0
    n = batch * seq_len
    bs = jnp.repeat(jnp.swapaxes(b_spatial, 1, 2), SGU_HEAD_DIM, axis=2)
    w_pool_rows = w_pool.reshape(depth, D_MODEL, POOL_GROUP_WIDTH)
    mixer_f32 = (w_in, w_pool_rows, w_branch_a, w_branch_b, w_out)
    ffn_f32 = (w_up, w_down, w_ple_gate, w_ple)
    mixer_scratch = [pltpu.VMEM((d // LANES, tm + POOL_HALO, LANES), jnp.float32)]
    ffn_scratch = [pltpu.VMEM((CONV_RING, tf + CONV_HALO, LANES), jnp.float32),
                   pltpu.VMEM((2 * D_FF // LANES, CONV_HALO, LANES), jnp.float32),
                   pltpu.VMEM((tf, D_FF), jnp.bfloat16)]
    xf = x.reshape(n, d)
    pf = p.reshape(depth * n, PLE_DIM)
    fg = final_norm.reshape(1, d)
    mixer_bf16 = [_weight(w[0]) for w in mixer_f32]
    for i in range(depth):
        last = i == depth - 1
        wi, wp, wa, wb, wo = mixer_bf16
        wp = wp.reshape(len(POOL_WINDOWS), POOL_GROUP_WIDTH, POOL_GROUP_WIDTH)
        params = (mix_norm, wi, wp, pool_scale, sgu_norm, w_spatial, bs, wa, wb, wo)
        specs = [_layer_block(q, i) if q.ndim > 2 and q.dtype == jnp.float32 else _whole(q)
                 for q in params]
        xf, (wu, wd, wg, wl) = _call(
            functools.partial(_mixer_kernel, seq_len, i), "mixer", tm, xf, (), [], params, specs,
            ffn_f32, i, mixer_scratch)
        params = (ffn_norm, wu, conv_w, conv_b, wd, ple_norm, wg, wl, fg)
        specs = [_layer_block(q, i) if q.ndim > 2 and q.dtype == jnp.float32 else _whole(q)
                 for q in params]
        p_tile = pl.BlockSpec((tf, PLE_DIM), lambda j, i=i: (i * (n // tf) + j, 0))
        xf, mixer_bf16 = _call(
            functools.partial(_ffn_kernel, seq_len, i, last), "ffn_final" if last else "ffn",
            tf, xf, (pf,), [p_tile], params, specs, () if last else mixer_f32, i + 1, ffn_scratch)
    return xf.reshape(batch, seq_len, d)
```

```python
import functools

import jax
import jax.numpy as jnp
import numpy as np
from jax import lax
from jax.experimental import pallas as pl
from jax.experimental.pallas import tpu as pltpu

LANES = 128
SUBLANES = 8
BF16_ROWS = 16
SLAB = 256
D_MODEL = 1024
N_SLABS = D_MODEL // SLAB
POOL_WINDOWS = (2, 4, 8, 16)
POOL_GROUP_WIDTH = SLAB
POOL_HALO = 32
SGU_CHUNK = 128
SGU_HEADS = 8
SGU_HEAD_DIM = D_MODEL // SGU_HEADS
D_FF = 2816
FF_SLABS = D_FF // SLAB
CONV_WIDTH = 3
CONV_HALO = 8
PLE_DIM = 256
EPS = 1e-6

TOKEN_TILE = 512
FFN_TOKEN_TILE = 1024
CONV_RING = 8
V7X_VMEM_LIMIT_BYTES = 58 * 1024 * 1024

_SQRT_HALF = float(np.sqrt(0.5))
_N_MIXER_IN = 11
_N_FFN_IN = 11


def _gelu_x2(x):
    return x * (1.0 + lax.erf(x * _SQRT_HALF))


def _sigmoid_x2(x):
    return jnp.tanh(0.5 * x) + 1.0


def _sigmoid(x):
    return 0.5 * _sigmoid_x2(x)


def _dot(a, b):
    return jnp.dot(a, b, preferred_element_type=jnp.float32)


def _bf16(x):
    return x.astype(jnp.bfloat16)


def _cat(slabs):
    return jnp.concatenate(slabs, axis=1)


def _rms_scale(slabs, eps=EPS):
    ssq = sum(jnp.sum(s * s, axis=-1, keepdims=True) for s in slabs)
    width = sum(s.shape[1] for s in slabs)
    return lax.rsqrt(ssq * (1.0 / width) + eps)


def _rmsnorm_bf16(x, g_ref):
    return _bf16(x * _rms_scale([x]) * g_ref[...])


def _col(ref, j):
    return ref[:, j * SLAB:(j + 1) * SLAB]


def _layer_row(ref, layer):
    return ref.at[pl.ds(layer, 1)]


def _window_sum(zp_ref, tmp_ref, s, z_self, w):
    tm = z_self.shape[0]
    stages = w.bit_length() - 1
    cur = None
    for i in range(stages):
        sh = 1 << i
        back = SUBLANES * (stages - 1 - i)
        lo, rows = POOL_HALO - back, tm + back
        if i == 0:
            a = z_self if back == 0 else zp_ref[s, lo:lo + rows, :]
            b = zp_ref[s, lo - sh:lo - sh + rows, :]
        elif sh % SUBLANES == 0:
            a, b = cur[sh:], cur[:rows]
        else:
            tmp_ref[i - 1, s, lo - SUBLANES:lo + rows, :] = cur
            a = cur[SUBLANES:]
            b = tmp_ref[i - 1, s, lo - sh:lo - sh + rows, :]
        cur = a + b
    return cur


def _cast_rows(src_refs, dst_refs):
    for src, dst in zip(src_refs, dst_refs):
        rows, n = src.shape
        dst[:, 0:n] = _bf16(src[...])
        if dst.shape[1] > n:
            dst[:, n:] = jnp.zeros((rows, dst.shape[1] - n), jnp.bfloat16)


def _mixer_kernel(seq_len, layer, n_cast, *refs):
    (x_ref, g_ref, w_in_ref, w_pool_ref, pscale_ref, sgu_g_ref, ws_ref, bs_ref,
     wa_ref, wb_ref, wo_ref) = refs[:_N_MIXER_IN]
    cast_in = refs[_N_MIXER_IN:_N_MIXER_IN + n_cast]
    o_ref = refs[_N_MIXER_IN + n_cast]
    cast_out = refs[_N_MIXER_IN + n_cast + 1:_N_MIXER_IN + 2 * n_cast + 1]
    zp_ref, tmp_ref = refs[_N_MIXER_IN + 2 * n_cast + 1:]
    g_ref, pscale_ref, sgu_g_ref = (_layer_row(r, layer) for r in (g_ref, pscale_ref, sgu_g_ref))
    tm = x_ref.shape[0]
    t0 = (pl.program_id(0) * tm) & (seq_len - 1)

    @pl.when(t0 == 0)
    def _():
        zp_ref[:, 0:POOL_HALO, :] = jnp.zeros((2 * N_SLABS, POOL_HALO, LANES), jnp.float32)

    x = x_ref[...]
    h = _rmsnorm_bf16(x, g_ref)

    slab_dots = lambda first: [_dot(h, _col(w_in_ref, first + j)) for j in range(N_SLABS)]

    z = slab_dots(0)
    u = slab_dots(N_SLABS)
    v = slab_dots(2 * N_SLABS)
    t_pos = lax.broadcasted_iota(jnp.int32, (tm, LANES), 0) + t0
    inv_t = 1.0 / (t_pos + 1).astype(jnp.float32)
    y_pool = []
    for gi, w in enumerate(POOL_WINDOWS):
        inv_cnt = jnp.maximum(inv_t, 1.0 / w)
        pooled = []
        for half in range(SLAB // LANES):
            s = 2 * gi + half
            z_self = z[gi][:, half * LANES:(half + 1) * LANES]
            zp_ref[s, POOL_HALO:POOL_HALO + tm, :] = z_self
            pooled.append(_window_sum(zp_ref, tmp_ref, s, z_self, w) * inv_cnt - z_self)
        y_pool.append(_dot(_bf16(_cat(pooled)), w_pool_ref[gi]) * _col(pscale_ref, gi))
    zp_ref[:, 0:POOL_HALO, :] = zp_ref[:, tm:tm + POOL_HALO, :]
    y_pool = _bf16(_cat(y_pool))
    g_a = slab_dots(3 * N_SLABS)
    y_a = [_dot(y_pool, _col(wa_ref, j)) for j in range(N_SLABS)]

    u = [_gelu_x2(s) for s in u]
    v = [_gelu_x2(s) for s in v]
    v_scale = _rms_scale(v, 4.0 * EPS)
    vn = [_bf16(v[j] * v_scale * _col(sgu_g_ref, j)) for j in range(N_SLABS)]
    row = lax.broadcasted_iota(jnp.int32, (SGU_CHUNK, SGU_CHUNK), 0)
    col = lax.broadcasted_iota(jnp.int32, (SGU_CHUNK, SGU_CHUNK), 1)
    tril = row >= col
    ws = [_bf16(jnp.where(tril, 0.5 * ws_ref[hh], 0.0)) for hh in range(SGU_HEADS)]
    mixed = []
    for j in range(N_SLABS):
        half_bias = 0.5 * _col(bs_ref, j)
        chunks = []
        for c in range(tm // SGU_CHUNK):
            rows = slice(c * SGU_CHUNK, (c + 1) * SGU_CHUNK)
            heads = []
            for half in range(SLAB // SGU_HEAD_DIM):
                hcols = slice(half * SGU_HEAD_DIM, (half + 1) * SGU_HEAD_DIM)
                heads.append(_dot(ws[2 * j + half], vn[j][rows, hcols]))
            chunks.append(_cat(heads) + half_bias)
        mixed.append(jnp.concatenate(chunks, axis=0))
    g_b = slab_dots(4 * N_SLABS)
    sgu = _bf16(_cat([u[j] * mixed[j] for j in range(N_SLABS)]))
    y_b = [_dot(sgu, _col(wb_ref, j)) for j in range(N_SLABS)]

    merged = _bf16(_cat([0.5 * (_sigmoid_x2(g_a[j]) * y_a[j] + _sigmoid_x2(g_b[j]) * y_b[j])
                         for j in range(N_SLABS)]))
    for j in range(N_SLABS):
        o_ref[:, j * SLAB:(j + 1) * SLAB] = _col(x_ref, j) + _dot(merged, _col(wo_ref, j))
    _cast_rows(cast_in, cast_out)


def _ffn_kernel(seq_len, layer, final, n_cast, *refs):
    (x_ref, p_ref, g_ref, w_up_ref, cw_ref, cb_ref, w_down_ref, pg_ref, w_gate_ref,
     w_ple_ref, fg_ref) = refs[:_N_FFN_IN]
    cast_in = refs[_N_FFN_IN:_N_FFN_IN + n_cast]
    o_ref = refs[_N_FFN_IN + n_cast]
    cast_out = refs[_N_FFN_IN + n_cast + 1:_N_FFN_IN + 2 * n_cast + 1]
    up_ref, carry_ref, act_ref = refs[_N_FFN_IN + 2 * n_cast + 1:]
    g_ref, cb_ref, pg_ref = (_layer_row(r, layer) for r in (g_ref, cb_ref, pg_ref))
    tm = x_ref.shape[0]
    t0 = (pl.program_id(0) * tm) & (seq_len - 1)

    @pl.when(t0 == 0)
    def _():
        carry_ref[...] = jnp.zeros(carry_ref.shape, jnp.float32)

    x = x_ref[...]
    h = _rmsnorm_bf16(x, g_ref)

    def conv(slab, scale):
        up = _dot(h, _col(w_up_ref, slab))
        halves = []
        for half in range(SLAB // LANES):
            s = 2 * slab + half
            cols = slice(s * LANES, (s + 1) * LANES)
            cur = up[:, half * LANES:(half + 1) * LANES]
            r = s % CONV_RING
            up_ref[r, 0:CONV_HALO, :] = carry_ref[s]
            up_ref[r, CONV_HALO:CONV_HALO + tm, :] = cur
            carry_ref[s] = cur[tm - CONV_HALO:, :]
            taps = [scale * cw_ref[k:k + 1, cols] for k in range(CONV_WIDTH)]
            out = scale * cb_ref[:, cols]
            for k in range(CONV_WIDTH - 1):
                lo = CONV_HALO - (CONV_WIDTH - 1) + k
                out = out + taps[k] * up_ref[r, lo:lo + tm, :]
            halves.append(out + taps[CONV_WIDTH - 1] * cur)
        return _cat(halves)

    for j in range(FF_SLABS):
        a = conv(j, 1.0)
        half_b = conv(FF_SLABS + j, 0.5)
        act_ref[:, j * SLAB:(j + 1) * SLAB] = _bf16(_gelu_x2(a) * half_b)
    act = act_ref[...]
    x = [_col(x_ref, j) + _dot(act, _col(w_down_ref, j)) for j in range(N_SLABS)]

    x_scale = _rms_scale(x)
    hp = _bf16(_cat([x[j] * x_scale * _col(pg_ref, j) for j in range(N_SLABS)]))
    pb = _bf16(p_ref[...])
    for j in range(N_SLABS):
        gate = _sigmoid(_dot(hp, _col(w_gate_ref, j)))
        x[j] = x[j] + gate * _dot(pb, _col(w_ple_ref, j))
    if final:
        x_scale = _rms_scale(x)
        x = [x[j] * x_scale * _col(fg_ref, j) for j in range(N_SLABS)]
    for j in range(N_SLABS):
        o_ref[:, j * SLAB:(j + 1) * SLAB] = x[j]
    _cast_rows(cast_in, cast_out)


def _layer_block(w, layer):
    rest = w.shape[1:]
    return pl.BlockSpec((None,) + rest, lambda i: (layer,) + (0,) * len(rest),
                        pipeline_mode=pl.Buffered(1))


def _whole(w):
    return pl.BlockSpec(w.shape, lambda i: (0,) * w.ndim, pipeline_mode=pl.Buffered(1))


def _padded_width(n):
    return n + (LANES if (n // LANES) % SUBLANES == 0 else 0)


def _cast_plan(w, layer, steps):
    _, k, n = w.shape
    rows = next(r for r in range(BF16_ROWS, k + 1, BF16_ROWS) if k % r == 0 and k // r <= steps)
    last = k // rows - 1
    npad = _padded_width(n)
    return (pl.BlockSpec((None, rows, n), lambda i: (layer, jnp.minimum(i, last), 0)),
            pl.BlockSpec((rows, npad), lambda i: (jnp.minimum(i, last), 0)),
            jax.ShapeDtypeStruct((k, npad), jnp.bfloat16))


def _compiler_params():
    return pltpu.CompilerParams(dimension_semantics=("arbitrary",),
                                vmem_limit_bytes=V7X_VMEM_LIMIT_BYTES)


def _call(body, name, tm, x, tile_inputs, tile_specs, params, param_specs, to_cast, cast_layer,
          scratch):
    n, d = x.shape
    steps = n // tm
    tile = pl.BlockSpec((tm, d), lambda i: (i, 0))
    plans = [_cast_plan(w, cast_layer, steps) for w in to_cast]
    out = pl.pallas_call(
        functools.partial(body, len(plans)),
        grid=(steps,),
        in_specs=[tile] + tile_specs + param_specs + [pl_in for pl_in, _, _ in plans],
        out_specs=[tile] + [pl_out for _, pl_out, _ in plans],
        out_shape=[jax.ShapeDtypeStruct((n, d), jnp.float32)] + [s for _, _, s in plans],
        scratch_shapes=scratch,
        compiler_params=_compiler_params(),
        name=name,
    )(x, *tile_inputs, *params, *to_cast)
    return out[0], out[1:]


def _weight(w):
    k, n = w.shape
    return jnp.zeros((k, _padded_width(n)), jnp.bfloat16).at[:, :n].set(_bf16(w))


def kernel(x, p, mix_norm, w_in, w_pool, pool_scale, sgu_norm, w_spatial, b_spatial, w_branch_a, w_branch_b, w_out, ffn_norm, w_up, conv_w, conv_b, w_down, ple_norm, w_ple_gate, w_ple, final_norm):
    batch, seq_len, d = x.shape
    depth = w_in.shape[0]
    tm, tf = TOKEN_TILE, FFN_TOKEN_TILE
    assert d == D_MODEL and seq_len % tm == 0 and seq_len % tf == 0
    assert seq_len & (seq_len - 1) == 0 and tm % SGU_CHUNK == 0
    n = batch * seq_len
    bs = jnp.repeat(jnp.swapaxes(b_spatial, 1, 2), SGU_HEAD_DIM, axis=2)
    w_pool_rows = w_pool.reshape(depth, D_MODEL, POOL_GROUP_WIDTH)
    mixer_f32 = (w_in, w_pool_rows, w_branch_a, w_branch_b, w_out)
    ffn_f32 = (w_up, w_down, w_ple_gate, w_ple)
    mixer_scratch = [pltpu.VMEM((d // LANES, tm + POOL_HALO, LANES), jnp.float32),
                     pltpu.VMEM((2, d // LANES, tm + POOL_HALO, LANES), jnp.float32)]
    ffn_scratch = [pltpu.VMEM((CONV_RING, tf + CONV_HALO, LANES), jnp.float32),
                   pltpu.VMEM((2 * D_FF // LANES, CONV_HALO, LANES), jnp.float32),
                   pltpu.VMEM((tf, D_FF), jnp.bfloat16)]
    xf = x.reshape(n, d)
    pf = p.reshape(depth * n, PLE_DIM)
    fg = final_norm.reshape(1, d)
    mixer_bf16 = [_weight(w[0]) for w in mixer_f32]
    for i in range(depth):
        last = i == depth - 1
        wi, wp, wa, wb, wo = mixer_bf16
        wp = wp.reshape(len(POOL_WINDOWS), POOL_GROUP_WIDTH, POOL_GROUP_WIDTH)
        params = (mix_norm, wi, wp, pool_scale, sgu_norm, w_spatial, bs, wa, wb, wo)
        specs = [_layer_block(q, i) if q.ndim > 2 and q.dtype == jnp.float32 else _whole(q)
                 for q in params]
        xf, (wu, wd, wg, wl) = _call(
            functools.partial(_mixer_kernel, seq_len, i), "mixer", tm, xf, (), [], params, specs,
            ffn_f32, i, mixer_scratch)
        params = (ffn_norm, wu, conv_w, conv_b, wd, ple_norm, wg, wl, fg)
        specs = [_layer_block(q, i) if q.ndim > 2 and q.dtype == jnp.float32 else _whole(q)
                 for q in params]
        p_tile = pl.BlockSpec((tf, PLE_DIM), lambda j, i=i: (i * (n // tf) + j, 0))
        xf, mixer_bf16 = _call(
            functools.partial(_ffn_kernel, seq_len, i, last), "ffn_final" if last else "ffn",
            tf, xf, (pf,), [p_tile], params, specs, () if last else mixer_f32, i + 1, ffn_scratch)
    return xf.reshape(batch, seq_len, d)
```

```python
import functools

import jax
import jax.numpy as jnp
import numpy as np
from jax import lax
from jax.experimental import pallas as pl
from jax.experimental.pallas import tpu as pltpu

LANES = 128
SUBLANES = 8
BF16_ROWS = 16
SLAB = 256
D_MODEL = 1024
N_SLABS = D_MODEL // SLAB
POOL_WINDOWS = (2, 4, 8, 16)
POOL_GROUP_WIDTH = SLAB
POOL_HALO = 32
SGU_CHUNK = 128
SGU_HEADS = 8
SGU_HEAD_DIM = D_MODEL // SGU_HEADS
D_FF = 2816
FF_SLABS = D_FF // SLAB
CONV_WIDTH = 3
CONV_HALO = 8
PLE_DIM = 256
EPS = 1e-6

TOKEN_TILE = 512
FFN_TOKEN_TILE = 1024
CONV_RING = 8
V7X_VMEM_LIMIT_BYTES = 58 * 1024 * 1024

_SQRT_HALF = float(np.sqrt(0.5))
_N_MIXER_IN = 11
_N_FFN_IN = 11


def _gelu_x2(x):
    return x * (1.0 + lax.erf(x * _SQRT_HALF))


def _sigmoid_x2(x):
    return jnp.tanh(0.5 * x) + 1.0


def _sigmoid(x):
    return 0.5 * _sigmoid_x2(x)


def _dot(a, b):
    return jnp.dot(a, b, preferred_element_type=jnp.float32)


def _bf16(x):
    return x.astype(jnp.bfloat16)


def _cat(slabs):
    return jnp.concatenate(slabs, axis=1)


def _rms_scale(slabs, eps=EPS):
    ssq = sum(jnp.sum(s * s, axis=-1, keepdims=True) for s in slabs)
    width = sum(s.shape[1] for s in slabs)
    return lax.rsqrt(ssq * (1.0 / width) + eps)


def _rmsnorm_bf16(x, g_ref):
    return _bf16(x * _rms_scale([x]) * g_ref[...])


def _col(ref, j):
    return ref[:, j * SLAB:(j + 1) * SLAB]


def _layer_row(ref, layer):
    return ref.at[pl.ds(layer, 1)]


def _window_sum(zp_ref, tmp_ref, s, z_self, w):
    tm = z_self.shape[0]
    stages = w.bit_length() - 1
    cur = None
    for i in range(stages):
        sh = 1 << i
        back = SUBLANES * (stages - 1 - i)
        lo, rows = POOL_HALO - back, tm + back
        if i == 0:
            a = z_self if back == 0 else zp_ref[s, lo:lo + rows, :]
            b = zp_ref[s, lo - sh:lo - sh + rows, :]
        elif sh % SUBLANES == 0:
            a, b = cur[sh:], cur[:rows]
        else:
            tmp_ref[i - 1, s, lo - SUBLANES:lo + rows, :] = cur
            a = cur[SUBLANES:]
            b = tmp_ref[i - 1, s, lo - sh:lo - sh + rows, :]
        cur = a + b
    return cur


def _cast_rows(src_refs, dst_refs):
    for src, dst in zip(src_refs, dst_refs):
        rows, n = src.shape
        dst[:, 0:n] = _bf16(src[...])
        if dst.shape[1] > n:
            dst[:, n:] = jnp.zeros((rows, dst.shape[1] - n), jnp.bfloat16)


def _mixer_kernel(seq_len, layer, n_cast, *refs):
    (x_ref, g_ref, w_in_ref, w_pool_ref, pscale_ref, sgu_g_ref, ws_ref, bs_ref,
     wa_ref, wb_ref, wo_ref) = refs[:_N_MIXER_IN]
    cast_in = refs[_N_MIXER_IN:_N_MIXER_IN + n_cast]
    o_ref = refs[_N_MIXER_IN + n_cast]
    cast_out = refs[_N_MIXER_IN + n_cast + 1:_N_MIXER_IN + 2 * n_cast + 1]
    zp_ref, tmp_ref, *lhs_refs = refs[_N_MIXER_IN + 2 * n_cast + 1:]
    g_ref, pscale_ref, sgu_g_ref = (_layer_row(r, layer) for r in (g_ref, pscale_ref, sgu_g_ref))
    tm = x_ref.shape[0]
    t0 = (pl.program_id(0) * tm) & (seq_len - 1)

    @pl.when(t0 == 0)
    def _():
        zp_ref[:, 0:POOL_HALO, :] = jnp.zeros((2 * N_SLABS, POOL_HALO, LANES), jnp.float32)

    x = x_ref[...]
    h = _rmsnorm_bf16(x, g_ref)
    same = jnp.minimum(pl.program_id(0), 0)

    def via_vmem(slabs, first):
        refs_ = lhs_refs[first:first + N_SLABS]
        for r, s in zip(refs_, slabs):
            r[0] = _bf16(s)
        return lambda: _cat([r[same] for r in refs_])

    slab_dots = lambda first: [_dot(h, _col(w_in_ref, first + j)) for j in range(N_SLABS)]

    z = slab_dots(0)
    u = slab_dots(N_SLABS)
    v = slab_dots(2 * N_SLABS)
    t_pos = lax.broadcasted_iota(jnp.int32, (tm, LANES), 0) + t0
    inv_t = 1.0 / (t_pos + 1).astype(jnp.float32)
    y_pool = []
    for gi, w in enumerate(POOL_WINDOWS):
        inv_cnt = jnp.maximum(inv_t, 1.0 / w)
        pooled = []
        for half in range(SLAB // LANES):
            s = 2 * gi + half
            z_self = z[gi][:, half * LANES:(half + 1) * LANES]
            zp_ref[s, POOL_HALO:POOL_HALO + tm, :] = z_self
            pooled.append(_window_sum(zp_ref, tmp_ref, s, z_self, w) * inv_cnt - z_self)
        y_pool.append(_dot(_bf16(_cat(pooled)), w_pool_ref[gi]) * _col(pscale_ref, gi))
    zp_ref[:, 0:POOL_HALO, :] = zp_ref[:, tm:tm + POOL_HALO, :]
    y_pool = via_vmem(y_pool, 0)
    g_a = slab_dots(3 * N_SLABS)
    y_a = [_dot(y_pool(), _col(wa_ref, j)) for j in range(N_SLABS)]

    u = [_gelu_x2(s) for s in u]
    v = [_gelu_x2(s) for s in v]
    v_scale = _rms_scale(v, 4.0 * EPS)
    vn = [_bf16(v[j] * v_scale * _col(sgu_g_ref, j)) for j in range(N_SLABS)]
    row = lax.broadcasted_iota(jnp.int32, (SGU_CHUNK, SGU_CHUNK), 0)
    col = lax.broadcasted_iota(jnp.int32, (SGU_CHUNK, SGU_CHUNK), 1)
    tril = row >= col
    ws = [_bf16(jnp.where(tril, 0.5 * ws_ref[hh], 0.0)) for hh in range(SGU_HEADS)]
    mixed = []
    for j in range(N_SLABS):
        half_bias = 0.5 * _col(bs_ref, j)
        chunks = []
        for c in range(tm // SGU_CHUNK):
            rows = slice(c * SGU_CHUNK, (c + 1) * SGU_CHUNK)
            heads = []
            for half in range(SLAB // SGU_HEAD_DIM):
                hcols = slice(half * SGU_HEAD_DIM, (half + 1) * SGU_HEAD_DIM)
                heads.append(_dot(ws[2 * j + half], vn[j][rows, hcols]))
            chunks.append(_cat(heads) + half_bias)
        mixed.append(jnp.concatenate(chunks, axis=0))
    g_b = slab_dots(4 * N_SLABS)
    sgu = via_vmem([u[j] * mixed[j] for j in range(N_SLABS)], N_SLABS)
    y_b = [_dot(sgu(), _col(wb_ref, j)) for j in range(N_SLABS)]

    merged = via_vmem([0.5 * (_sigmoid_x2(g_a[j]) * y_a[j] + _sigmoid_x2(g_b[j]) * y_b[j])
                       for j in range(N_SLABS)], 2 * N_SLABS)
    for j in range(N_SLABS):
        o_ref[:, j * SLAB:(j + 1) * SLAB] = _col(x_ref, j) + _dot(merged(), _col(wo_ref, j))
    _cast_rows(cast_in, cast_out)


def _ffn_kernel(seq_len, layer, final, n_cast, *refs):
    (x_ref, p_ref, g_ref, w_up_ref, cw_ref, cb_ref, w_down_ref, pg_ref, w_gate_ref,
     w_ple_ref, fg_ref) = refs[:_N_FFN_IN]
    cast_in = refs[_N_FFN_IN:_N_FFN_IN + n_cast]
    o_ref = refs[_N_FFN_IN + n_cast]
    cast_out = refs[_N_FFN_IN + n_cast + 1:_N_FFN_IN + 2 * n_cast + 1]
    up_ref, carry_ref, act_ref = refs[_N_FFN_IN + 2 * n_cast + 1:]
    g_ref, cb_ref, pg_ref = (_layer_row(r, layer) for r in (g_ref, cb_ref, pg_ref))
    tm = x_ref.shape[0]
    t0 = (pl.program_id(0) * tm) & (seq_len - 1)

    @pl.when(t0 == 0)
    def _():
        carry_ref[...] = jnp.zeros(carry_ref.shape, jnp.float32)

    x = x_ref[...]
    h = _rmsnorm_bf16(x, g_ref)

    def conv(slab, scale):
        up = _dot(h, _col(w_up_ref, slab))
        halves = []
        for half in range(SLAB // LANES):
            s = 2 * slab + half
            cols = slice(s * LANES, (s + 1) * LANES)
            cur = up[:, half * LANES:(half + 1) * LANES]
            r = s % CONV_RING
            up_ref[r, 0:CONV_HALO, :] = carry_ref[s]
            up_ref[r, CONV_HALO:CONV_HALO + tm, :] = cur
            carry_ref[s] = cur[tm - CONV_HALO:, :]
            taps = [scale * cw_ref[k:k + 1, cols] for k in range(CONV_WIDTH)]
            out = scale * cb_ref[:, cols]
            for k in range(CONV_WIDTH - 1):
                lo = CONV_HALO - (CONV_WIDTH - 1) + k
                out = out + taps[k] * up_ref[r, lo:lo + tm, :]
            halves.append(out + taps[CONV_WIDTH - 1] * cur)
        return _cat(halves)

    for j in range(FF_SLABS):
        a = conv(j, 1.0)
        half_b = conv(FF_SLABS + j, 0.5)
        act_ref[:, j * SLAB:(j + 1) * SLAB] = _bf16(_gelu_x2(a) * half_b)
    act = act_ref[...]
    x = [_col(x_ref, j) + _dot(act, _col(w_down_ref, j)) for j in range(N_SLABS)]

    x_scale = _rms_scale(x)
    hp = _bf16(_cat([x[j] * x_scale * _col(pg_ref, j) for j in range(N_SLABS)]))
    pb = _bf16(p_ref[...])
    for j in range(N_SLABS):
        gate = _sigmoid(_dot(hp, _col(w_gate_ref, j)))
        x[j] = x[j] + gate * _dot(pb, _col(w_ple_ref, j))
    if final:
        x_scale = _rms_scale(x)
        x = [x[j] * x_scale * _col(fg_ref, j) for j in range(N_SLABS)]
    for j in range(N_SLABS):
        o_ref[:, j * SLAB:(j + 1) * SLAB] = x[j]
    _cast_rows(cast_in, cast_out)


def _layer_block(w, layer):
    rest = w.shape[1:]
    return pl.BlockSpec((None,) + rest, lambda i: (layer,) + (0,) * len(rest),
                        pipeline_mode=pl.Buffered(1))


def _whole(w):
    return pl.BlockSpec(w.shape, lambda i: (0,) * w.ndim, pipeline_mode=pl.Buffered(1))


def _padded_width(n):
    return n + (LANES if (n // LANES) % SUBLANES == 0 else 0)


def _cast_plan(w, layer, steps):
    _, k, n = w.shape
    rows = next(r for r in range(BF16_ROWS, k + 1, BF16_ROWS) if k % r == 0 and k // r <= steps)
    last = k // rows - 1
    npad = _padded_width(n)
    return (pl.BlockSpec((None, rows, n), lambda i: (layer, jnp.minimum(i, last), 0)),
            pl.BlockSpec((rows, npad), lambda i: (jnp.minimum(i, last), 0)),
            jax.ShapeDtypeStruct((k, npad), jnp.bfloat16))


def _compiler_params():
    return pltpu.CompilerParams(dimension_semantics=("arbitrary",),
                                vmem_limit_bytes=V7X_VMEM_LIMIT_BYTES)


def _call(body, name, tm, x, tile_inputs, tile_specs, params, param_specs, to_cast, cast_layer,
          scratch):
    n, d = x.shape
    steps = n // tm
    tile = pl.BlockSpec((tm, d), lambda i: (i, 0))
    plans = [_cast_plan(w, cast_layer, steps) for w in to_cast]
    out = pl.pallas_call(
        functools.partial(body, len(plans)),
        grid=(steps,),
        in_specs=[tile] + tile_specs + param_specs + [pl_in for pl_in, _, _ in plans],
        out_specs=[tile] + [pl_out for _, pl_out, _ in plans],
        out_shape=[jax.ShapeDtypeStruct((n, d), jnp.float32)] + [s for _, _, s in plans],
        scratch_shapes=scratch,
        compiler_params=_compiler_params(),
        name=name,
    )(x, *tile_inputs, *params, *to_cast)
    return out[0], out[1:]


def _weight(w):
    k, n = w.shape
    return jnp.zeros((k, _padded_width(n)), jnp.bfloat16).at[:, :n].set(_bf16(w))


def kernel(x, p, mix_norm, w_in, w_pool, pool_scale, sgu_norm, w_spatial, b_spatial, w_branch_a, w_branch_b, w_out, ffn_norm, w_up, conv_w, conv_b, w_down, ple_norm, w_ple_gate, w_ple, final_norm):
    batch, seq_len, d = x.shape
    depth = w_in.shape[0]
    tm, tf = TOKEN_TILE, FFN_TOKEN_TILE
    assert d == D_MODEL and seq_len % tm == 0 and seq_len % tf == 0
    assert seq_len & (seq_len - 1) == 0 and tm % SGU_CHUNK == 0
    n = batch * seq_len
    bs = jnp.repeat(jnp.swapaxes(b_spatial, 1, 2), SGU_HEAD_DIM, axis=2)
    w_pool_rows = w_pool.reshape(depth, D_MODEL, POOL_GROUP_WIDTH)
    mixer_f32 = (w_in, w_pool_rows, w_branch_a, w_branch_b, w_out)
    ffn_f32 = (w_up, w_down, w_ple_gate, w_ple)
    mixer_scratch = [pltpu.VMEM((d // LANES, tm + POOL_HALO, LANES), jnp.float32),
                     pltpu.VMEM((2, d // LANES, tm + POOL_HALO, LANES), jnp.float32),
                     ] + [pltpu.VMEM((1, tm, SLAB), jnp.bfloat16)] * (3 * N_SLABS)
    ffn_scratch = [pltpu.VMEM((CONV_RING, tf + CONV_HALO, LANES), jnp.float32),
                   pltpu.VMEM((2 * D_FF // LANES, CONV_HALO, LANES), jnp.float32),
                   pltpu.VMEM((tf, D_FF), jnp.bfloat16)]
    xf = x.reshape(n, d)
    pf = p.reshape(depth * n, PLE_DIM)
    fg = final_norm.reshape(1, d)
    mixer_bf16 = [_weight(w[0]) for w in mixer_f32]
    for i in range(depth):
        last = i == depth - 1
        wi, wp, wa, wb, wo = mixer_bf16
        wp = wp.reshape(len(POOL_WINDOWS), POOL_GROUP_WIDTH, POOL_GROUP_WIDTH)
        params = (mix_norm, wi, wp, pool_scale, sgu_norm, w_spatial, bs, wa, wb, wo)
        specs = [_layer_block(q, i) if q.ndim > 2 and q.dtype == jnp.float32 else _whole(q)
                 for q in params]
        xf, (wu, wd, wg, wl) = _call(
            functools.partial(_mixer_kernel, seq_len, i), "mixer", tm, xf, (), [], params, specs,
            ffn_f32, i, mixer_scratch)
        params = (ffn_norm, wu, conv_w, conv_b, wd, ple_norm, wg, wl, fg)
        specs = [_layer_block(q, i) if q.ndim > 2 and q.dtype == jnp.float32 else _whole(q)
                 for q in params]
        p_tile = pl.BlockSpec((tf, PLE_DIM), lambda j, i=i: (i * (n // tf) + j, 0))
        xf, mixer_bf16 = _call(
            functools.partial(_ffn_kernel, seq_len, i, last), "ffn_final" if last else "ffn",
            tf, xf, (pf,), [p_tile], params, specs, () if last else mixer_f32, i + 1, ffn_scratch)
    return xf.reshape(batch, seq_len, d)
```

```python
import functools

import jax
import jax.numpy as jnp
import numpy as np
from jax import lax
from jax.experimental import pallas as pl
from jax.experimental.pallas import tpu as pltpu

LANES = 128
SUBLANES = 8
BF16_ROWS = 16
SLAB = 256
D_MODEL = 1024
N_SLABS = D_MODEL // SLAB
POOL_WINDOWS = (2, 4, 8, 16)
POOL_GROUP_WIDTH = SLAB
POOL_HALO = 32
SGU_CHUNK = 128
SGU_HEADS = 8
SGU_HEAD_DIM = D_MODEL // SGU_HEADS
D_FF = 2816
FF_SLABS = D_FF // SLAB
CONV_WIDTH = 3
CONV_HALO = 8
PLE_DIM = 256
EPS = 1e-6

TOKEN_TILE = 512
FFN_TOKEN_TILE = 1024
CONV_RING = 8
V7X_VMEM_LIMIT_BYTES = 58 * 1024 * 1024

_SQRT_HALF = float(np.sqrt(0.5))
_N_MIXER_IN = 11
_N_FFN_IN = 11


def _gelu_x2(x):
    return x * (1.0 + lax.erf(x * _SQRT_HALF))


def _sigmoid_x2(x):
    return jnp.tanh(0.5 * x) + 1.0


def _sigmoid(x):
    return 0.5 * _sigmoid_x2(x)


def _dot(a, b):
    return jnp.dot(a, b, preferred_element_type=jnp.float32)


def _bf16(x):
    return x.astype(jnp.bfloat16)


def _cat(slabs):
    return jnp.concatenate(slabs, axis=1)


def _rms_scale(slabs, eps=EPS):
    ssq = sum(jnp.sum(s * s, axis=-1, keepdims=True) for s in slabs)
    width = sum(s.shape[1] for s in slabs)
    return lax.rsqrt(ssq * (1.0 / width) + eps)


def _rmsnorm_bf16(x, g_ref):
    return _bf16(x * _rms_scale([x]) * g_ref[...])


def _col(ref, j):
    return ref[:, j * SLAB:(j + 1) * SLAB]


def _layer_row(ref, layer):
    return ref.at[pl.ds(layer, 1)]


def _window_sum(zp_ref, tmp_ref, s, z_self, w):
    tm = z_self.shape[0]
    stages = w.bit_length() - 1
    cur = None
    for i in range(stages):
        sh = 1 << i
        back = SUBLANES * (stages - 1 - i)
        lo, rows = POOL_HALO - back, tm + back
        if i == 0:
            a = z_self if back == 0 else zp_ref[s, lo:lo + rows, :]
            b = zp_ref[s, lo - sh:lo - sh + rows, :]
        elif sh % SUBLANES == 0:
            a, b = cur[sh:], cur[:rows]
        else:
            tmp_ref[i - 1, s, lo - SUBLANES:lo + rows, :] = cur
            a = cur[SUBLANES:]
            b = tmp_ref[i - 1, s, lo - sh:lo - sh + rows, :]
        cur = a + b
    return cur


def _cast_rows(src_refs, dst_refs):
    for src, dst in zip(src_refs, dst_refs):
        rows, n = src.shape
        dst[:, 0:n] = _bf16(src[...])
        if dst.shape[1] > n:
            dst[:, n:] = jnp.zeros((rows, dst.shape[1] - n), jnp.bfloat16)


def _mixer_kernel(seq_len, layer, n_cast, *refs):
    (x_ref, g_ref, w_in_ref, w_pool_ref, pscale_ref, sgu_g_ref, ws_ref, bs_ref,
     wa_ref, wb_ref, wo_ref) = refs[:_N_MIXER_IN]
    cast_in = refs[_N_MIXER_IN:_N_MIXER_IN + n_cast]
    o_ref = refs[_N_MIXER_IN + n_cast]
    cast_out = refs[_N_MIXER_IN + n_cast + 1:_N_MIXER_IN + 2 * n_cast + 1]
    zp_ref, tmp_ref = refs[_N_MIXER_IN + 2 * n_cast + 1:]
    g_ref, pscale_ref, sgu_g_ref = (_layer_row(r, layer) for r in (g_ref, pscale_ref, sgu_g_ref))
    tm = x_ref.shape[0]
    t0 = (pl.program_id(0) * tm) & (seq_len - 1)

    @pl.when(t0 == 0)
    def _():
        zp_ref[:, 0:POOL_HALO, :] = jnp.zeros((2 * N_SLABS, POOL_HALO, LANES), jnp.float32)

    x = x_ref[...]
    h = _rmsnorm_bf16(x, g_ref)

    slab_dots = lambda first: [_dot(h, _col(w_in_ref, first + j)) for j in range(N_SLABS)]

    z = slab_dots(0)
    u = slab_dots(N_SLABS)
    v = slab_dots(2 * N_SLABS)
    t_pos = lax.broadcasted_iota(jnp.int32, (tm, LANES), 0) + t0
    inv_t = 1.0 / (t_pos + 1).astype(jnp.float32)
    y_pool = []
    for gi, w in enumerate(POOL_WINDOWS):
        inv_cnt = jnp.maximum(inv_t, 1.0 / w)
        pooled = []
        for half in range(SLAB // LANES):
            s = 2 * gi + half
            z_self = z[gi][:, half * LANES:(half + 1) * LANES]
            zp_ref[s, POOL_HALO:POOL_HALO + tm, :] = z_self
            pooled.append(_window_sum(zp_ref, tmp_ref, s, z_self, w) * inv_cnt - z_self)
        y_pool.append(_dot(_bf16(_cat(pooled)), w_pool_ref[gi]) * _col(pscale_ref, gi))
    zp_ref[:, 0:POOL_HALO, :] = zp_ref[:, tm:tm + POOL_HALO, :]
    y_pool = _bf16(_cat(y_pool))
    g_a = slab_dots(3 * N_SLABS)
    y_a = [_dot(y_pool, _col(wa_ref, j)) for j in range(N_SLABS)]

    u = [_gelu_x2(s) for s in u]
    v = [_gelu_x2(s) for s in v]
    v_scale = _rms_scale(v, 4.0 * EPS)
    vn = [_bf16(v[j] * v_scale * _col(sgu_g_ref, j)) for j in range(N_SLABS)]
    row = lax.broadcasted_iota(jnp.int32, (SGU_CHUNK, SGU_CHUNK), 0)
    col = lax.broadcasted_iota(jnp.int32, (SGU_CHUNK, SGU_CHUNK), 1)
    tril = row >= col
    ws = [_bf16(jnp.where(tril, 0.5 * ws_ref[hh], 0.0)) for hh in range(SGU_HEADS)]
    mixed = []
    for j in range(N_SLABS):
        half_bias = 0.5 * _col(bs_ref, j)
        chunks = []
        for c in range(tm // SGU_CHUNK):
            rows = slice(c * SGU_CHUNK, (c + 1) * SGU_CHUNK)
            heads = []
            for half in range(SLAB // SGU_HEAD_DIM):
                hcols = slice(half * SGU_HEAD_DIM, (half + 1) * SGU_HEAD_DIM)
                heads.append(_dot(ws[2 * j + half], vn[j][rows, hcols]))
            chunks.append(_cat(heads) + half_bias)
        mixed.append(jnp.concatenate(chunks, axis=0))
    g_b = slab_dots(4 * N_SLABS)
    sgu = _bf16(_cat([u[j] * mixed[j] for j in range(N_SLABS)]))
    y_b = [_dot(sgu, _col(wb_ref, j)) for j in range(N_SLABS)]

    merged = _bf16(_cat([0.5 * (_sigmoid_x2(g_a[j]) * y_a[j] + _sigmoid_x2(g_b[j]) * y_b[j])
                         for j in range(N_SLABS)]))
    for j in range(N_SLABS):
        o_ref[:, j * SLAB:(j + 1) * SLAB] = _col(x_ref, j) + _dot(merged, _col(wo_ref, j))
    _cast_rows(cast_in, cast_out)


def _ffn_kernel(seq_len, layer, final, n_cast, *refs):
    (x_ref, p_ref, g_ref, w_up_ref, cw_ref, cb_ref, w_down_ref, pg_ref, w_gate_ref,
     w_ple_ref, fg_ref) = refs[:_N_FFN_IN]
    cast_in = refs[_N_FFN_IN:_N_FFN_IN + n_cast]
    o_ref = refs[_N_FFN_IN + n_cast]
    cast_out = refs[_N_FFN_IN + n_cast + 1:_N_FFN_IN + 2 * n_cast + 1]
    up_ref, carry_ref, act_ref = refs[_N_FFN_IN + 2 * n_cast + 1:]
    g_ref, cb_ref, pg_ref = (_layer_row(r, layer) for r in (g_ref, cb_ref, pg_ref))
    tm = x_ref.shape[0]
    t0 = (pl.program_id(0) * tm) & (seq_len - 1)

    @pl.when(t0 == 0)
    def _():
        carry_ref[...] = jnp.zeros(carry_ref.shape, jnp.float32)

    x = x_ref[...]
    h = _rmsnorm_bf16(x, g_ref)

    def conv(slab, scale):
        up = _dot(h, _col(w_up_ref, slab))
        halves = []
        for half in range(SLAB // LANES):
            s = 2 * slab + half
            cols = slice(s * LANES, (s + 1) * LANES)
            cur = up[:, half * LANES:(half + 1) * LANES]
            r = s % CONV_RING
            up_ref[r, 0:CONV_HALO, :] = carry_ref[s]
            up_ref[r, CONV_HALO:CONV_HALO + tm, :] = cur
            carry_ref[s] = cur[tm - CONV_HALO:, :]
            taps = [scale * cw_ref[k:k + 1, cols] for k in range(CONV_WIDTH)]
            out = scale * cb_ref[:, cols]
            for k in range(CONV_WIDTH - 1):
                lo = CONV_HALO - (CONV_WIDTH - 1) + k
                out = out + taps[k] * up_ref[r, lo:lo + tm, :]
            halves.append(out + taps[CONV_WIDTH - 1] * cur)
        return _cat(halves)

    for j in range(FF_SLABS):
        a = conv(j, 1.0)
        half_b = conv(FF_SLABS + j, 0.5)
        act_ref[:, j * SLAB:(j + 1) * SLAB] = _bf16(_gelu_x2(a) * half_b)
    act = act_ref[...]
    x = [_col(x_ref, j) + _dot(act, _col(w_down_ref, j)) for j in range(N_SLABS)]

    x_scale = _rms_scale(x)
    hp = _bf16(_cat([x[j] * x_scale * _col(pg_ref, j) for j in range(N_SLABS)]))
    pb = _bf16(p_ref[...])
    for j in range(N_SLABS):
        gate = _sigmoid(_dot(hp, _col(w_gate_ref, j)))
        x[j] = x[j] + gate * _dot(pb, _col(w_ple_ref, j))
    if final:
        x_scale = _rms_scale(x)
        x = [x[j] * x_scale * _col(fg_ref, j) for j in range(N_SLABS)]
    for j in range(N_SLABS):
        o_ref[:, j * SLAB:(j + 1) * SLAB] = x[j]
    _cast_rows(cast_in, cast_out)


def _layer_block(w, layer):
    rest = w.shape[1:]
    return pl.BlockSpec((None,) + rest, lambda i: (layer,) + (0,) * len(rest),
                        pipeline_mode=pl.Buffered(1))


def _whole(w):
    return pl.BlockSpec(w.shape, lambda i: (0,) * w.ndim, pipeline_mode=pl.Buffered(1))


def _padded_width(n):
    return n + (LANES if (n // LANES) % SUBLANES == 0 else 0)


def _cast_plan(w, layer, steps):
    _, k, n = w.shape
    rows = next(r for r in range(BF16_ROWS, k + 1, BF16_ROWS) if k % r == 0 and k // r <= steps)
    last = k // rows - 1
    npad = _padded_width(n)
    return (pl.BlockSpec((None, rows, n), lambda i: (layer, jnp.minimum(i, last), 0)),
            pl.BlockSpec((rows, npad), lambda i: (jnp.minimum(i, last), 0)),
            jax.ShapeDtypeStruct((k, npad), jnp.bfloat16))


def _compiler_params():
    return pltpu.CompilerParams(dimension_semantics=("arbitrary",),
                                vmem_limit_bytes=V7X_VMEM_LIMIT_BYTES)


def _call(body, name, tm, x, tile_inputs, tile_specs, params, param_specs, to_cast, cast_layer,
          scratch):
    n, d = x.shape
    steps = n // tm
    tile = pl.BlockSpec((tm, d), lambda i: (i, 0))
    plans = [_cast_plan(w, cast_layer, steps) for w in to_cast]
    out = pl.pallas_call(
        functools.partial(body, len(plans)),
        grid=(steps,),
        in_specs=[tile] + tile_specs + param_specs + [pl_in for pl_in, _, _ in plans],
        out_specs=[tile] + [pl_out for _, pl_out, _ in plans],
        out_shape=[jax.ShapeDtypeStruct((n, d), jnp.float32)] + [s for _, _, s in plans],
        scratch_shapes=scratch,
        compiler_params=_compiler_params(),
        name=name,
    )(x, *tile_inputs, *params, *to_cast)
    return out[0], out[1:]


def _weight(w):
    k, n = w.shape
    pad = _padded_width(n) - n
    return jnp.concatenate([_bf16(w), jnp.zeros((k, pad), jnp.bfloat16)], axis=1) if pad else _bf16(w)


def kernel(x, p, mix_norm, w_in, w_pool, pool_scale, sgu_norm, w_spatial, b_spatial, w_branch_a, w_branch_b, w_out, ffn_norm, w_up, conv_w, conv_b, w_down, ple_norm, w_ple_gate, w_ple, final_norm):
    batch, seq_len, d = x.shape
    depth = w_in.shape[0]
    tm, tf = TOKEN_TILE, FFN_TOKEN_TILE
    assert d == D_MODEL and seq_len % tm == 0 and seq_len % tf == 0
    assert seq_len & (seq_len - 1) == 0 and tm % SGU_CHUNK == 0
    n = batch * seq_len
    bs = jnp.repeat(jnp.swapaxes(b_spatial, 1, 2), SGU_HEAD_DIM, axis=2)
    w_pool_rows = w_pool.reshape(depth, D_MODEL, POOL_GROUP_WIDTH)
    mixer_f32 = (w_in, w_pool_rows, w_branch_a, w_branch_b, w_out)
    ffn_f32 = (w_up, w_down, w_ple_gate, w_ple)
    mixer_scratch = [pltpu.VMEM((d // LANES, tm + POOL_HALO, LANES), jnp.float32),
                     pltpu.VMEM((2, d // LANES, tm + POOL_HALO, LANES), jnp.float32)]
    ffn_scratch = [pltpu.VMEM((CONV_RING, tf + CONV_HALO, LANES), jnp.float32),
                   pltpu.VMEM((2 * D_FF // LANES, CONV_HALO, LANES), jnp.float32),
                   pltpu.VMEM((tf, D_FF), jnp.bfloat16)]
    xf = x.reshape(n, d)
    pf = p.reshape(depth * n, PLE_DIM)
    fg = final_norm.reshape(1, d)
    mixer_bf16 = [_weight(w[0]) for w in mixer_f32]
    for i in range(depth):
        last = i == depth - 1
        wi, wp, wa, wb, wo = mixer_bf16
        wp = wp.reshape(len(POOL_WINDOWS), POOL_GROUP_WIDTH, POOL_GROUP_WIDTH)
        params = (mix_norm, wi, wp, pool_scale, sgu_norm, w_spatial, bs, wa, wb, wo)
        specs = [_layer_block(q, i) if q.ndim > 2 and q.dtype == jnp.float32 else _whole(q)
                 for q in params]
        xf, (wu, wd, wg, wl) = _call(
            functools.partial(_mixer_kernel, seq_len, i), "mixer", tm, xf, (), [], params, specs,
            ffn_f32, i, mixer_scratch)
        params = (ffn_norm, wu, conv_w, conv_b, wd, ple_norm, wg, wl, fg)
        specs = [_layer_block(q, i) if q.ndim > 2 and q.dtype == jnp.float32 else _whole(q)
                 for q in params]
        p_tile = pl.BlockSpec((tf, PLE_DIM), lambda j, i=i: (i * (n // tf) + j, 0))
        xf, mixer_bf16 = _call(
            functools.partial(_ffn_kernel, seq_len, i, last), "ffn_final" if last else "ffn",
            tf, xf, (pf,), [p_tile], params, specs, () if last else mixer_f32, i + 1, ffn_scratch)
    return xf.reshape(batch, seq_len, d)
```

```python
import functools

import jax
import jax.numpy as jnp
import numpy as np
from jax import lax
from jax.experimental import pallas as pl
from jax.experimental.pallas import tpu as pltpu

LANES = 128
SUBLANES = 8
BF16_ROWS = 16
SLAB = 256
D_MODEL = 1024
N_SLABS = D_MODEL // SLAB
POOL_WINDOWS = (2, 4, 8, 16)
POOL_GROUP_WIDTH = SLAB
POOL_HALO = 32
SGU_CHUNK = 128
SGU_HEADS = 8
SGU_HEAD_DIM = D_MODEL // SGU_HEADS
D_FF = 2816
FF_SLABS = D_FF // SLAB
CONV_WIDTH = 3
CONV_HALO = 8
PLE_DIM = 256
EPS = 1e-6

TOKEN_TILE = 512
FFN_TOKEN_TILE = 1024
CONV_RING = 8
V7X_VMEM_LIMIT_BYTES = 58 * 1024 * 1024

_SQRT_HALF = float(np.sqrt(0.5))
_N_MIXER_IN = 11
_N_FFN_IN = 11


def _gelu_x2(x):
    return x * (1.0 + lax.erf(x * _SQRT_HALF))


def _sigmoid_x2(x):
    return jnp.tanh(0.5 * x) + 1.0


def _sigmoid(x):
    return 0.5 * _sigmoid_x2(x)


def _dot(a, b):
    return jnp.dot(a, b, preferred_element_type=jnp.float32)


def _bf16(x):
    return x.astype(jnp.bfloat16)


def _cat(slabs):
    return jnp.concatenate(slabs, axis=1)


def _rms_scale(slabs, eps=EPS):
    ssq = sum(jnp.sum(s * s, axis=-1, keepdims=True) for s in slabs)
    width = sum(s.shape[1] for s in slabs)
    return lax.rsqrt(ssq * (1.0 / width) + eps)


def _rmsnorm_bf16(x, g_ref):
    return _bf16(x * _rms_scale([x]) * g_ref[...])


def _col(ref, j):
    return ref[:, j * SLAB:(j + 1) * SLAB]


def _layer_row(ref, layer):
    return ref.at[pl.ds(layer, 1)]


def _window_sum(zp_ref, tmp_ref, s, z_self, w):
    tm = z_self.shape[0]
    stages = w.bit_length() - 1
    cur = None
    for i in range(stages):
        sh = 1 << i
        back = SUBLANES * (stages - 1 - i)
        lo, rows = POOL_HALO - back, tm + back
        if i == 0:
            a = z_self if back == 0 else zp_ref[s, lo:lo + rows, :]
            b = zp_ref[s, lo - sh:lo - sh + rows, :]
        elif sh % SUBLANES == 0:
            a, b = cur[sh:], cur[:rows]
        else:
            tmp_ref[i - 1, s, lo - SUBLANES:lo + rows, :] = cur
            a = cur[SUBLANES:]
            b = tmp_ref[i - 1, s, lo - sh:lo - sh + rows, :]
        cur = a + b
    return cur


def _cast_rows(src_refs, dst_refs):
    for src, dst in zip(src_refs, dst_refs):
        rows, n = src.shape
        dst[:, 0:n] = _bf16(src[...])
        if dst.shape[1] > n:
            dst[:, n:] = jnp.zeros((rows, dst.shape[1] - n), jnp.bfloat16)


def _mixer_kernel(seq_len, layer, n_cast, *refs):
    (x_ref, g_ref, w_in_ref, w_pool_ref, pscale_ref, sgu_g_ref, ws_ref, bs_ref,
     wa_ref, wb_ref, wo_ref) = refs[:_N_MIXER_IN]
    cast_in = refs[_N_MIXER_IN:_N_MIXER_IN + n_cast]
    o_ref = refs[_N_MIXER_IN + n_cast]
    cast_out = refs[_N_MIXER_IN + n_cast + 1:_N_MIXER_IN + 2 * n_cast + 1]
    zp_ref, tmp_ref = refs[_N_MIXER_IN + 2 * n_cast + 1:]
    g_ref, pscale_ref, sgu_g_ref = (_layer_row(r, layer) for r in (g_ref, pscale_ref, sgu_g_ref))
    tm = x_ref.shape[0]
    t0 = (pl.program_id(0) * tm) & (seq_len - 1)

    @pl.when(t0 == 0)
    def _():
        zp_ref[:, 0:POOL_HALO, :] = jnp.zeros((2 * N_SLABS, POOL_HALO, LANES), jnp.float32)

    x = x_ref[...]
    h = _rmsnorm_bf16(x, g_ref)

    slab_dots = lambda first: [_dot(h, _col(w_in_ref, first + j)) for j in range(N_SLABS)]

    z = slab_dots(0)
    u = slab_dots(N_SLABS)
    v = slab_dots(2 * N_SLABS)
    t_pos = lax.broadcasted_iota(jnp.int32, (tm, LANES), 0) + t0
    inv_t = 1.0 / (t_pos + 1).astype(jnp.float32)
    y_pool = []
    for gi, w in enumerate(POOL_WINDOWS):
        inv_cnt = jnp.maximum(inv_t, 1.0 / w)
        pooled = []
        for half in range(SLAB // LANES):
            s = 2 * gi + half
            z_self = z[gi][:, half * LANES:(half + 1) * LANES]
            zp_ref[s, POOL_HALO:POOL_HALO + tm, :] = z_self
            pooled.append(_window_sum(zp_ref, tmp_ref, s, z_self, w) * inv_cnt - z_self)
        y_pool.append(_dot(_bf16(_cat(pooled)), w_pool_ref[gi]) * _col(pscale_ref, gi))
    zp_ref[:, 0:POOL_HALO, :] = zp_ref[:, tm:tm + POOL_HALO, :]
    y_pool = _bf16(_cat(y_pool))
    g_a = slab_dots(3 * N_SLABS)
    y_a = [_dot(y_pool, _col(wa_ref, j)) for j in range(N_SLABS)]

    u = [_gelu_x2(s) for s in u]
    v = [_gelu_x2(s) for s in v]
    v_scale = _rms_scale(v, 4.0 * EPS)
    vn = [_bf16(v[j] * v_scale * _col(sgu_g_ref, j)) for j in range(N_SLABS)]
    row = lax.broadcasted_iota(jnp.int32, (SGU_CHUNK, SGU_CHUNK), 0)
    col = lax.broadcasted_iota(jnp.int32, (SGU_CHUNK, SGU_CHUNK), 1)
    tril = row >= col
    ws = [_bf16(jnp.where(tril, 0.5 * ws_ref[hh], 0.0)) for hh in range(SGU_HEADS)]
    mixed = []
    for j in range(N_SLABS):
        half_bias = 0.5 * _col(bs_ref, j)
        chunks = []
        for c in range(tm // SGU_CHUNK):
            rows = slice(c * SGU_CHUNK, (c + 1) * SGU_CHUNK)
            heads = []
            for half in range(SLAB // SGU_HEAD_DIM):
                hcols = slice(half * SGU_HEAD_DIM, (half + 1) * SGU_HEAD_DIM)
                heads.append(_dot(ws[2 * j + half], vn[j][rows, hcols]))
            chunks.append(_cat(heads) + half_bias)
        mixed.append(jnp.concatenate(chunks, axis=0))
    g_b = slab_dots(4 * N_SLABS)
    sgu = _bf16(_cat([u[j] * mixed[j] for j in range(N_SLABS)]))
    y_b = [_dot(sgu, _col(wb_ref, j)) for j in range(N_SLABS)]

    merged = _bf16(_cat([0.5 * (_sigmoid_x2(g_a[j]) * y_a[j] + _sigmoid_x2(g_b[j]) * y_b[j])
                         for j in range(N_SLABS)]))
    for j in range(N_SLABS):
        o_ref[:, j * SLAB:(j + 1) * SLAB] = _col(x_ref, j) + _dot(merged, _col(wo_ref, j))
    _cast_rows(cast_in, cast_out)


def _ffn_kernel(seq_len, layer, final, n_cast, *refs):
    (x_ref, p_ref, g_ref, w_up_ref, cw_ref, cb_ref, w_down_ref, pg_ref, w_gate_ref,
     w_ple_ref, fg_ref) = refs[:_N_FFN_IN]
    cast_in = refs[_N_FFN_IN:_N_FFN_IN + n_cast]
    o_ref = refs[_N_FFN_IN + n_cast]
    cast_out = refs[_N_FFN_IN + n_cast + 1:_N_FFN_IN + 2 * n_cast + 1]
    up_ref, carry_ref, act_ref = refs[_N_FFN_IN + 2 * n_cast + 1:]
    g_ref, cb_ref, pg_ref = (_layer_row(r, layer) for r in (g_ref, cb_ref, pg_ref))
    tm = x_ref.shape[0]
    t0 = (pl.program_id(0) * tm) & (seq_len - 1)

    @pl.when(t0 == 0)
    def _():
        carry_ref[...] = jnp.zeros(carry_ref.shape, jnp.float32)

    x = x_ref[...]
    h = _rmsnorm_bf16(x, g_ref)

    def conv(slab):
        up = _dot(h, _col(w_up_ref, slab))
        halves = []
        for half in range(SLAB // LANES):
            s = 2 * slab + half
            cols = slice(s * LANES, (s + 1) * LANES)
            cur = up[:, half * LANES:(half + 1) * LANES]
            r = s % CONV_RING
            up_ref[r, 0:CONV_HALO, :] = carry_ref[s]
            up_ref[r, CONV_HALO:CONV_HALO + tm, :] = cur
            carry_ref[s] = cur[tm - CONV_HALO:, :]
            out = cb_ref[:, cols]
            for k in range(CONV_WIDTH - 1):
                lo = CONV_HALO - (CONV_WIDTH - 1) + k
                out = out + cw_ref[k:k + 1, cols] * up_ref[r, lo:lo + tm, :]
            halves.append(out + cw_ref[CONV_WIDTH - 1:CONV_WIDTH, cols] * cur)
        return _cat(halves)

    for j in range(FF_SLABS):
        a = conv(j)
        b = conv(FF_SLABS + j)
        act_ref[:, j * SLAB:(j + 1) * SLAB] = _bf16((0.5 * _gelu_x2(a)) * b)
    act = act_ref[...]
    x = [_col(x_ref, j) + _dot(act, _col(w_down_ref, j)) for j in range(N_SLABS)]

    x_scale = _rms_scale(x)
    hp = _bf16(_cat([x[j] * x_scale * _col(pg_ref, j) for j in range(N_SLABS)]))
    pb = _bf16(p_ref[...])
    for j in range(N_SLABS):
        gate = _sigmoid(_dot(hp, _col(w_gate_ref, j)))
        x[j] = x[j] + gate * _dot(pb, _col(w_ple_ref, j))
    if final:
        x_scale = _rms_scale(x)
        x = [x[j] * x_scale * _col(fg_ref, j) for j in range(N_SLABS)]
    for j in range(N_SLABS):
        o_ref[:, j * SLAB:(j + 1) * SLAB] = x[j]
    _cast_rows(cast_in, cast_out)


def _layer_block(w, layer):
    rest = w.shape[1:]
    return pl.BlockSpec((None,) + rest, lambda i: (layer,) + (0,) * len(rest),
                        pipeline_mode=pl.Buffered(1))


def _whole(w):
    return pl.BlockSpec(w.shape, lambda i: (0,) * w.ndim, pipeline_mode=pl.Buffered(1))


def _padded_width(n):
    return n + (LANES if (n // LANES) % SUBLANES == 0 else 0)


def _cast_plan(w, layer, steps):
    _, k, n = w.shape
    rows = next(r for r in range(BF16_ROWS, k + 1, BF16_ROWS) if k % r == 0 and k // r <= steps)
    last = k // rows - 1
    npad = _padded_width(n)
    return (pl.BlockSpec((None, rows, n), lambda i: (layer, jnp.minimum(i, last), 0)),
            pl.BlockSpec((rows, npad), lambda i: (jnp.minimum(i, last), 0)),
            jax.ShapeDtypeStruct((k, npad), jnp.bfloat16))


def _compiler_params():
    return pltpu.CompilerParams(dimension_semantics=("arbitrary",),
                                vmem_limit_bytes=V7X_VMEM_LIMIT_BYTES)


def _call(body, name, tm, x, tile_inputs, tile_specs, params, param_specs, to_cast, cast_layer,
          scratch):
    n, d = x.shape
    steps = n // tm
    tile = pl.BlockSpec((tm, d), lambda i: (i, 0))
    plans = [_cast_plan(w, cast_layer, steps) for w in to_cast]
    out = pl.pallas_call(
        functools.partial(body, len(plans)),
        grid=(steps,),
        in_specs=[tile] + tile_specs + param_specs + [pl_in for pl_in, _, _ in plans],
        out_specs=[tile] + [pl_out for _, pl_out, _ in plans],
        out_shape=[jax.ShapeDtypeStruct((n, d), jnp.float32)] + [s for _, _, s in plans],
        scratch_shapes=scratch,
        compiler_params=_compiler_params(),
        name=name,
    )(x, *tile_inputs, *params, *to_cast)
    return out[0], out[1:]


def _weight(w):
    k, n = w.shape
    return jnp.zeros((k, _padded_width(n)), jnp.bfloat16).at[:, :n].set(_bf16(w))


def kernel(x, p, mix_norm, w_in, w_pool, pool_scale, sgu_norm, w_spatial, b_spatial, w_branch_a, w_branch_b, w_out, ffn_norm, w_up, conv_w, conv_b, w_down, ple_norm, w_ple_gate, w_ple, final_norm):
    batch, seq_len, d = x.shape
    depth = w_in.shape[0]
    tm, tf = TOKEN_TILE, FFN_TOKEN_TILE
    assert d == D_MODEL and seq_len % tm == 0 and seq_len % tf == 0
    assert seq_len & (seq_len - 1) == 0 and tm % SGU_CHUNK == 0
    n = batch * seq_len
    bs = jnp.repeat(jnp.swapaxes(b_spatial, 1, 2), SGU_HEAD_DIM, axis=2)
    w_pool_rows = w_pool.reshape(depth, D_MODEL, POOL_GROUP_WIDTH)
    mixer_f32 = (w_in, w_pool_rows, w_branch_a, w_branch_b, w_out)
    ffn_f32 = (w_up, w_down, w_ple_gate, w_ple)
    mixer_scratch = [pltpu.VMEM((d // LANES, tm + POOL_HALO, LANES), jnp.float32),
                     pltpu.VMEM((2, d // LANES, tm + POOL_HALO, LANES), jnp.float32)]
    ffn_scratch = [pltpu.VMEM((CONV_RING, tf + CONV_HALO, LANES), jnp.float32),
                   pltpu.VMEM((2 * D_FF // LANES, CONV_HALO, LANES), jnp.float32),
                   pltpu.VMEM((tf, D_FF), jnp.bfloat16)]
    xf = x.reshape(n, d)
    pf = p.reshape(depth * n, PLE_DIM)
    fg = final_norm.reshape(1, d)
    mixer_bf16 = [_weight(w[0]) for w in mixer_f32]
    for i in range(depth):
        last = i == depth - 1
        wi, wp, wa, wb, wo = mixer_bf16
        wp = wp.reshape(len(POOL_WINDOWS), POOL_GROUP_WIDTH, POOL_GROUP_WIDTH)
        params = (mix_norm, wi, wp, pool_scale, sgu_norm, w_spatial, bs, wa, wb, wo)
        specs = [_layer_block(q, i) if q.ndim > 2 and q.dtype == jnp.float32 else _whole(q)
                 for q in params]
        xf, (wu, wd, wg, wl) = _call(
            functools.partial(_mixer_kernel, seq_len, i), "mixer", tm, xf, (), [], params, specs,
            ffn_f32, i, mixer_scratch)
        params = (ffn_norm, wu, conv_w, conv_b, wd, ple_norm, wg, wl, fg)
        specs = [_layer_block(q, i) if q.ndim > 2 and q.dtype == jnp.float32 else _whole(q)
                 for q in params]
        p_tile = pl.BlockSpec((tf, PLE_DIM), lambda j, i=i: (i * (n // tf) + j, 0))
        xf, mixer_bf16 = _call(
            functools.partial(_ffn_kernel, seq_len, i, last), "ffn_final" if last else "ffn",
            tf, xf, (pf,), [p_tile], params, specs, () if last else mixer_f32, i + 1, ffn_scratch)
    return xf.reshape(batch, seq_len, d)
```

```python
import functools

import jax
import jax.numpy as jnp
import numpy as np
from jax import lax
from jax.experimental import pallas as pl
from jax.experimental.pallas import tpu as pltpu

LANES = 128
SUBLANES = 8
BF16_ROWS = 16
SLAB = 256
D_MODEL = 1024
N_SLABS = D_MODEL // SLAB
POOL_WINDOWS = (2, 4, 8, 16)
POOL_GROUP_WIDTH = SLAB
POOL_HALO = 32
SGU_CHUNK = 128
SGU_HEADS = 8
SGU_HEAD_DIM = D_MODEL // SGU_HEADS
D_FF = 2816
FF_SLABS = D_FF // SLAB
CONV_WIDTH = 3
CONV_HALO = 8
PLE_DIM = 256
EPS = 1e-6

TOKEN_TILE = 512
FFN_TOKEN_TILE = 1024
CONV_RING = 8
V7X_VMEM_LIMIT_BYTES = 58 * 1024 * 1024

_SQRT_HALF = float(np.sqrt(0.5))
_N_MIXER_IN = 11
_N_FFN_IN = 11
CAST_STEPS = 8


def _gelu_x2(x):
    return x * (1.0 + lax.erf(x * _SQRT_HALF))


def _sigmoid_x2(x):
    return jnp.tanh(0.5 * x) + 1.0


def _sigmoid(x):
    return 0.5 * _sigmoid_x2(x)


def _dot(a, b):
    return jnp.dot(a, b, preferred_element_type=jnp.float32)


def _bf16(x):
    return x.astype(jnp.bfloat16)


def _cat(slabs):
    return jnp.concatenate(slabs, axis=1)


def _rms_scale(slabs, eps=EPS):
    ssq = sum(jnp.sum(s * s, axis=-1, keepdims=True) for s in slabs)
    width = sum(s.shape[1] for s in slabs)
    return lax.rsqrt(ssq * (1.0 / width) + eps)


def _rmsnorm_bf16(x, g_ref):
    return _bf16(x * _rms_scale([x]) * g_ref[...])


def _col(ref, j):
    return ref[:, j * SLAB:(j + 1) * SLAB]


def _layer_row(ref, layer):
    return ref.at[pl.ds(layer, 1)]


def _window_sum(zp_ref, tmp_ref, s, z_self, w):
    tm = z_self.shape[0]
    stages = w.bit_length() - 1
    cur = None
    for i in range(stages):
        sh = 1 << i
        back = SUBLANES * (stages - 1 - i)
        lo, rows = POOL_HALO - back, tm + back
        if i == 0:
            a = z_self if back == 0 else zp_ref[s, lo:lo + rows, :]
            b = zp_ref[s, lo - sh:lo - sh + rows, :]
        elif sh % SUBLANES == 0:
            a, b = cur[sh:], cur[:rows]
        else:
            tmp_ref[i - 1, s, lo - SUBLANES:lo + rows, :] = cur
            a = cur[SUBLANES:]
            b = tmp_ref[i - 1, s, lo - sh:lo - sh + rows, :]
        cur = a + b
    return cur


def _cast_kernel(n_cast, *refs):
    _cast_rows(refs[:n_cast], refs[n_cast:])


def _cast_rows(src_refs, dst_refs):
    for src, dst in zip(src_refs, dst_refs):
        rows, n = src.shape
        dst[:, 0:n] = _bf16(src[...])
        if dst.shape[1] > n:
            dst[:, n:] = jnp.zeros((rows, dst.shape[1] - n), jnp.bfloat16)


def _mixer_kernel(seq_len, layer, n_cast, *refs):
    (x_ref, g_ref, w_in_ref, w_pool_ref, pscale_ref, sgu_g_ref, ws_ref, bs_ref,
     wa_ref, wb_ref, wo_ref) = refs[:_N_MIXER_IN]
    cast_in = refs[_N_MIXER_IN:_N_MIXER_IN + n_cast]
    o_ref = refs[_N_MIXER_IN + n_cast]
    cast_out = refs[_N_MIXER_IN + n_cast + 1:_N_MIXER_IN + 2 * n_cast + 1]
    zp_ref, tmp_ref = refs[_N_MIXER_IN + 2 * n_cast + 1:]
    g_ref, pscale_ref, sgu_g_ref = (_layer_row(r, layer) for r in (g_ref, pscale_ref, sgu_g_ref))
    tm = x_ref.shape[0]
    t0 = (pl.program_id(0) * tm) & (seq_len - 1)

    @pl.when(t0 == 0)
    def _():
        zp_ref[:, 0:POOL_HALO, :] = jnp.zeros((2 * N_SLABS, POOL_HALO, LANES), jnp.float32)

    x = x_ref[...]
    h = _rmsnorm_bf16(x, g_ref)

    slab_dots = lambda first: [_dot(h, _col(w_in_ref, first + j)) for j in range(N_SLABS)]

    z = slab_dots(0)
    u = slab_dots(N_SLABS)
    v = slab_dots(2 * N_SLABS)
    t_pos = lax.broadcasted_iota(jnp.int32, (tm, LANES), 0) + t0
    inv_t = 1.0 / (t_pos + 1).astype(jnp.float32)
    y_pool = []
    for gi, w in enumerate(POOL_WINDOWS):
        inv_cnt = jnp.maximum(inv_t, 1.0 / w)
        pooled = []
        for half in range(SLAB // LANES):
            s = 2 * gi + half
            z_self = z[gi][:, half * LANES:(half + 1) * LANES]
            zp_ref[s, POOL_HALO:POOL_HALO + tm, :] = z_self
            pooled.append(_window_sum(zp_ref, tmp_ref, s, z_self, w) * inv_cnt - z_self)
        y_pool.append(_dot(_bf16(_cat(pooled)), w_pool_ref[gi]) * _col(pscale_ref, gi))
    zp_ref[:, 0:POOL_HALO, :] = zp_ref[:, tm:tm + POOL_HALO, :]
    y_pool = _bf16(_cat(y_pool))
    g_a = slab_dots(3 * N_SLABS)
    y_a = [_dot(y_pool, _col(wa_ref, j)) for j in range(N_SLABS)]

    u = [_gelu_x2(s) for s in u]
    v = [_gelu_x2(s) for s in v]
    v_scale = _rms_scale(v, 4.0 * EPS)
    vn = [_bf16(v[j] * v_scale * _col(sgu_g_ref, j)) for j in range(N_SLABS)]
    row = lax.broadcasted_iota(jnp.int32, (SGU_CHUNK, SGU_CHUNK), 0)
    col = lax.broadcasted_iota(jnp.int32, (SGU_CHUNK, SGU_CHUNK), 1)
    tril = row >= col
    ws = [_bf16(jnp.where(tril, 0.5 * ws_ref[hh], 0.0)) for hh in range(SGU_HEADS)]
    mixed = []
    for j in range(N_SLABS):
        half_bias = 0.5 * _col(bs_ref, j)
        chunks = []
        for c in range(tm // SGU_CHUNK):
            rows = slice(c * SGU_CHUNK, (c + 1) * SGU_CHUNK)
            heads = []
            for half in range(SLAB // SGU_HEAD_DIM):
                hcols = slice(half * SGU_HEAD_DIM, (half + 1) * SGU_HEAD_DIM)
                heads.append(_dot(ws[2 * j + half], vn[j][rows, hcols]))
            chunks.append(_cat(heads) + half_bias)
        mixed.append(jnp.concatenate(chunks, axis=0))
    g_b = slab_dots(4 * N_SLABS)
    sgu = _bf16(_cat([u[j] * mixed[j] for j in range(N_SLABS)]))
    y_b = [_dot(sgu, _col(wb_ref, j)) for j in range(N_SLABS)]

    merged = _bf16(_cat([0.5 * (_sigmoid_x2(g_a[j]) * y_a[j] + _sigmoid_x2(g_b[j]) * y_b[j])
                         for j in range(N_SLABS)]))
    for j in range(N_SLABS):
        o_ref[:, j * SLAB:(j + 1) * SLAB] = _col(x_ref, j) + _dot(merged, _col(wo_ref, j))
    _cast_rows(cast_in, cast_out)


def _ffn_kernel(seq_len, layer, final, n_cast, *refs):
    (x_ref, p_ref, g_ref, w_up_ref, cw_ref, cb_ref, w_down_ref, pg_ref, w_gate_ref,
     w_ple_ref, fg_ref) = refs[:_N_FFN_IN]
    cast_in = refs[_N_FFN_IN:_N_FFN_IN + n_cast]
    o_ref = refs[_N_FFN_IN + n_cast]
    cast_out = refs[_N_FFN_IN + n_cast + 1:_N_FFN_IN + 2 * n_cast + 1]
    up_ref, carry_ref, act_ref = refs[_N_FFN_IN + 2 * n_cast + 1:]
    g_ref, cb_ref, pg_ref = (_layer_row(r, layer) for r in (g_ref, cb_ref, pg_ref))
    tm = x_ref.shape[0]
    t0 = (pl.program_id(0) * tm) & (seq_len - 1)

    @pl.when(t0 == 0)
    def _():
        carry_ref[...] = jnp.zeros(carry_ref.shape, jnp.float32)

    x = x_ref[...]
    h = _rmsnorm_bf16(x, g_ref)

    def conv(slab):
        up = _dot(h, _col(w_up_ref, slab))
        halves = []
        for half in range(SLAB // LANES):
            s = 2 * slab + half
            cols = slice(s * LANES, (s + 1) * LANES)
            cur = up[:, half * LANES:(half + 1) * LANES]
            r = s % CONV_RING
            up_ref[r, 0:CONV_HALO, :] = carry_ref[s]
            up_ref[r, CONV_HALO:CONV_HALO + tm, :] = cur
            carry_ref[s] = cur[tm - CONV_HALO:, :]
            out = cb_ref[:, cols]
            for k in range(CONV_WIDTH - 1):
                lo = CONV_HALO - (CONV_WIDTH - 1) + k
                out = out + cw_ref[k:k + 1, cols] * up_ref[r, lo:lo + tm, :]
            halves.append(out + cw_ref[CONV_WIDTH - 1:CONV_WIDTH, cols] * cur)
        return _cat(halves)

    for j in range(FF_SLABS):
        a = conv(j)
        b = conv(FF_SLABS + j)
        act_ref[:, j * SLAB:(j + 1) * SLAB] = _bf16((0.5 * _gelu_x2(a)) * b)
    act = act_ref[...]
    x = [_col(x_ref, j) + _dot(act, _col(w_down_ref, j)) for j in range(N_SLABS)]

    x_scale = _rms_scale(x)
    hp = _bf16(_cat([x[j] * x_scale * _col(pg_ref, j) for j in range(N_SLABS)]))
    pb = _bf16(p_ref[...])
    for j in range(N_SLABS):
        gate = _sigmoid(_dot(hp, _col(w_gate_ref, j)))
        x[j] = x[j] + gate * _dot(pb, _col(w_ple_ref, j))
    if final:
        x_scale = _rms_scale(x)
        x = [x[j] * x_scale * _col(fg_ref, j) for j in range(N_SLABS)]
    for j in range(N_SLABS):
        o_ref[:, j * SLAB:(j + 1) * SLAB] = x[j]
    _cast_rows(cast_in, cast_out)


def _layer_block(w, layer):
    rest = w.shape[1:]
    return pl.BlockSpec((None,) + rest, lambda i: (layer,) + (0,) * len(rest),
                        pipeline_mode=pl.Buffered(1))


def _whole(w):
    return pl.BlockSpec(w.shape, lambda i: (0,) * w.ndim, pipeline_mode=pl.Buffered(1))


def _padded_width(n):
    return n + (LANES if (n // LANES) % SUBLANES == 0 else 0)


def _cast_plan(w, layer, steps):
    _, k, n = w.shape
    rows = next(r for r in range(BF16_ROWS, k + 1, BF16_ROWS) if k % r == 0 and k // r <= steps)
    last = k // rows - 1
    npad = _padded_width(n)
    return (pl.BlockSpec((None, rows, n), lambda i: (layer, jnp.minimum(i, last), 0)),
            pl.BlockSpec((rows, npad), lambda i: (jnp.minimum(i, last), 0)),
            jax.ShapeDtypeStruct((k, npad), jnp.bfloat16))


def _compiler_params():
    return pltpu.CompilerParams(dimension_semantics=("arbitrary",),
                                vmem_limit_bytes=V7X_VMEM_LIMIT_BYTES)


def _call(body, name, tm, x, tile_inputs, tile_specs, params, param_specs, to_cast, cast_layer,
          scratch):
    n, d = x.shape
    steps = n // tm
    tile = pl.BlockSpec((tm, d), lambda i: (i, 0))
    plans = [_cast_plan(w, cast_layer, steps) for w in to_cast]
    out = pl.pallas_call(
        functools.partial(body, len(plans)),
        grid=(steps,),
        in_specs=[tile] + tile_specs + param_specs + [pl_in for pl_in, _, _ in plans],
        out_specs=[tile] + [pl_out for _, pl_out, _ in plans],
        out_shape=[jax.ShapeDtypeStruct((n, d), jnp.float32)] + [s for _, _, s in plans],
        scratch_shapes=scratch,
        compiler_params=_compiler_params(),
        name=name,
    )(x, *tile_inputs, *params, *to_cast)
    return out[0], out[1:]


def _cast_call(weights, layer):
    plans = [_cast_plan(w, layer, CAST_STEPS) for w in weights]
    return pl.pallas_call(
        functools.partial(_cast_kernel, len(plans)),
        grid=(CAST_STEPS,),
        in_specs=[p_in for p_in, _, _ in plans],
        out_specs=[p_out for _, p_out, _ in plans],
        out_shape=[s for _, _, s in plans],
        compiler_params=_compiler_params(),
        name="cast",
    )(*weights)


def kernel(x, p, mix_norm, w_in, w_pool, pool_scale, sgu_norm, w_spatial, b_spatial, w_branch_a, w_branch_b, w_out, ffn_norm, w_up, conv_w, conv_b, w_down, ple_norm, w_ple_gate, w_ple, final_norm):
    batch, seq_len, d = x.shape
    depth = w_in.shape[0]
    tm, tf = TOKEN_TILE, FFN_TOKEN_TILE
    assert d == D_MODEL and seq_len % tm == 0 and seq_len % tf == 0
    assert seq_len & (seq_len - 1) == 0 and tm % SGU_CHUNK == 0
    n = batch * seq_len
    bs = jnp.repeat(jnp.swapaxes(b_spatial, 1, 2), SGU_HEAD_DIM, axis=2)
    w_pool_rows = w_pool.reshape(depth, D_MODEL, POOL_GROUP_WIDTH)
    mixer_f32 = (w_in, w_pool_rows, w_branch_a, w_branch_b, w_out)
    ffn_f32 = (w_up, w_down, w_ple_gate, w_ple)
    mixer_scratch = [pltpu.VMEM((d // LANES, tm + POOL_HALO, LANES), jnp.float32),
                     pltpu.VMEM((2, d // LANES, tm + POOL_HALO, LANES), jnp.float32)]
    ffn_scratch = [pltpu.VMEM((CONV_RING, tf + CONV_HALO, LANES), jnp.float32),
                   pltpu.VMEM((2 * D_FF // LANES, CONV_HALO, LANES), jnp.float32),
                   pltpu.VMEM((tf, D_FF), jnp.bfloat16)]
    xf = x.reshape(n, d)
    pf = p.reshape(depth * n, PLE_DIM)
    fg = final_norm.reshape(1, d)
    mixer_bf16 = _cast_call(mixer_f32, 0)
    for i in range(depth):
        last = i == depth - 1
        wi, wp, wa, wb, wo = mixer_bf16
        wp = wp.reshape(len(POOL_WINDOWS), POOL_GROUP_WIDTH, POOL_GROUP_WIDTH)
        params = (mix_norm, wi, wp, pool_scale, sgu_norm, w_spatial, bs, wa, wb, wo)
        specs = [_layer_block(q, i) if q.ndim > 2 and q.dtype == jnp.float32 else _whole(q)
                 for q in params]
        xf, (wu, wd, wg, wl) = _call(
            functools.partial(_mixer_kernel, seq_len, i), "mixer", tm, xf, (), [], params, specs,
            ffn_f32, i, mixer_scratch)
        params = (ffn_norm, wu, conv_w, conv_b, wd, ple_norm, wg, wl, fg)
        specs = [_layer_block(q, i) if q.ndim > 2 and q.dtype == jnp.float32 else _whole(q)
                 for q in params]
        p_tile = pl.BlockSpec((tf, PLE_DIM), lambda j, i=i: (i * (n // tf) + j, 0))
        xf, mixer_bf16 = _call(
            functools.partial(_ffn_kernel, seq_len, i, last), "ffn_final" if last else "ffn",
            tf, xf, (pf,), [p_tile], params, specs, () if last else mixer_f32, i + 1, ffn_scratch)
    return xf.reshape(batch, seq_len, d)
```

```python
import functools

import jax
import jax.numpy as jnp
import numpy as np
from jax import lax
from jax.experimental import pallas as pl
from jax.experimental.pallas import tpu as pltpu

LANES = 128
SUBLANES = 8
BF16_ROWS = 16
SLAB = 256
D_MODEL = 1024
N_SLABS = D_MODEL // SLAB
POOL_WINDOWS = (2, 4, 8, 16)
POOL_GROUP_WIDTH = SLAB
POOL_HALO = 32
SGU_CHUNK = 128
SGU_HEADS = 8
SGU_HEAD_DIM = D_MODEL // SGU_HEADS
D_FF = 2816
FF_SLABS = D_FF // SLAB
CONV_WIDTH = 3
CONV_HALO = 8
PLE_DIM = 256
EPS = 1e-6

TOKEN_TILE = 512
FFN_TOKEN_TILE = 1024
CONV_RING = 8
V7X_VMEM_LIMIT_BYTES = 58 * 1024 * 1024

_SQRT_HALF = float(np.sqrt(0.5))
MIXER_HALF_FROM = (3 * D_MODEL, None, None, None, 0)
_N_MIXER_IN = 11
_N_FFN_IN = 11
CAST_STEPS = 8


def _gelu_x2(x):
    return x * (1.0 + lax.erf(x * _SQRT_HALF))


def _sigmoid_x2(half_x):
    return jnp.tanh(half_x) + 1.0


def _sigmoid(x):
    return 0.5 * _sigmoid_x2(0.5 * x)


def _dot(a, b):
    return jnp.dot(a, b, preferred_element_type=jnp.float32)


def _bf16(x):
    return x.astype(jnp.bfloat16)


def _cat(slabs):
    return jnp.concatenate(slabs, axis=1)


def _rms_scale(slabs, eps=EPS):
    ssq = sum(jnp.sum(s * s, axis=-1, keepdims=True) for s in slabs)
    width = sum(s.shape[1] for s in slabs)
    return lax.rsqrt(ssq * (1.0 / width) + eps)


def _rmsnorm_bf16(x, g_ref):
    return _bf16(x * _rms_scale([x]) * g_ref[...])


def _col(ref, j):
    return ref[:, j * SLAB:(j + 1) * SLAB]


def _layer_row(ref, layer):
    return ref.at[pl.ds(layer, 1)]


def _window_sum(zp_ref, tmp_ref, s, z_self, w):
    tm = z_self.shape[0]
    stages = w.bit_length() - 1
    cur = None
    for i in range(stages):
        sh = 1 << i
        back = SUBLANES * (stages - 1 - i)
        lo, rows = POOL_HALO - back, tm + back
        if i == 0:
            a = z_self if back == 0 else zp_ref[s, lo:lo + rows, :]
            b = zp_ref[s, lo - sh:lo - sh + rows, :]
        elif sh % SUBLANES == 0:
            a, b = cur[sh:], cur[:rows]
        else:
            tmp_ref[i - 1, s, lo - SUBLANES:lo + rows, :] = cur
            a = cur[SUBLANES:]
            b = tmp_ref[i - 1, s, lo - sh:lo - sh + rows, :]
        cur = a + b
    return cur


def _cast_kernel(half_from, *refs):
    n_cast = len(half_from)
    _cast_rows(refs[:n_cast], refs[n_cast:], half_from)


def _cast_rows(src_refs, dst_refs, half_from):
    for src, dst, h in zip(src_refs, dst_refs, half_from):
        rows, n = src.shape
        h = n if h is None else h
        if h > 0:
            dst[:, 0:h] = _bf16(src[:, 0:h])
        if h < n:
            dst[:, h:n] = _bf16(0.5 * src[:, h:n])
        if dst.shape[1] > n:
            dst[:, n:] = jnp.zeros((rows, dst.shape[1] - n), jnp.bfloat16)


def _mixer_kernel(seq_len, layer, n_cast, *refs):
    (x_ref, g_ref, w_in_ref, w_pool_ref, pscale_ref, sgu_g_ref, ws_ref, bs_ref,
     wa_ref, wb_ref, wo_ref) = refs[:_N_MIXER_IN]
    cast_in = refs[_N_MIXER_IN:_N_MIXER_IN + n_cast]
    o_ref = refs[_N_MIXER_IN + n_cast]
    cast_out = refs[_N_MIXER_IN + n_cast + 1:_N_MIXER_IN + 2 * n_cast + 1]
    zp_ref, tmp_ref = refs[_N_MIXER_IN + 2 * n_cast + 1:]
    g_ref, pscale_ref, sgu_g_ref = (_layer_row(r, layer) for r in (g_ref, pscale_ref, sgu_g_ref))
    tm = x_ref.shape[0]
    t0 = (pl.program_id(0) * tm) & (seq_len - 1)

    @pl.when(t0 == 0)
    def _():
        zp_ref[:, 0:POOL_HALO, :] = jnp.zeros((2 * N_SLABS, POOL_HALO, LANES), jnp.float32)

    x = x_ref[...]
    h = _rmsnorm_bf16(x, g_ref)

    slab_dots = lambda first: [_dot(h, _col(w_in_ref, first + j)) for j in range(N_SLABS)]

    z = slab_dots(0)
    u = slab_dots(N_SLABS)
    v = slab_dots(2 * N_SLABS)
    t_pos = lax.broadcasted_iota(jnp.int32, (tm, LANES), 0) + t0
    inv_t = 1.0 / (t_pos + 1).astype(jnp.float32)
    y_pool = []
    for gi, w in enumerate(POOL_WINDOWS):
        inv_cnt = jnp.maximum(inv_t, 1.0 / w)
        pooled = []
        for half in range(SLAB // LANES):
            s = 2 * gi + half
            z_self = z[gi][:, half * LANES:(half + 1) * LANES]
            zp_ref[s, POOL_HALO:POOL_HALO + tm, :] = z_self
            pooled.append(_window_sum(zp_ref, tmp_ref, s, z_self, w) * inv_cnt - z_self)
        y_pool.append(_dot(_bf16(_cat(pooled)), w_pool_ref[gi]) * _col(pscale_ref, gi))
    zp_ref[:, 0:POOL_HALO, :] = zp_ref[:, tm:tm + POOL_HALO, :]
    y_pool = _bf16(_cat(y_pool))
    g_a = slab_dots(3 * N_SLABS)
    y_a = [_dot(y_pool, _col(wa_ref, j)) for j in range(N_SLABS)]

    u = [_gelu_x2(s) for s in u]
    v = [_gelu_x2(s) for s in v]
    v_scale = _rms_scale(v, 4.0 * EPS)
    vn = [_bf16(v[j] * v_scale * _col(sgu_g_ref, j)) for j in range(N_SLABS)]
    row = lax.broadcasted_iota(jnp.int32, (SGU_CHUNK, SGU_CHUNK), 0)
    col = lax.broadcasted_iota(jnp.int32, (SGU_CHUNK, SGU_CHUNK), 1)
    tril = row >= col
    ws = [_bf16(jnp.where(tril, 0.5 * ws_ref[hh], 0.0)) for hh in range(SGU_HEADS)]
    mixed = []
    for j in range(N_SLABS):
        half_bias = 0.5 * _col(bs_ref, j)
        chunks = []
        for c in range(tm // SGU_CHUNK):
            rows = slice(c * SGU_CHUNK, (c + 1) * SGU_CHUNK)
            heads = []
            for half in range(SLAB // SGU_HEAD_DIM):
                hcols = slice(half * SGU_HEAD_DIM, (half + 1) * SGU_HEAD_DIM)
                heads.append(_dot(ws[2 * j + half], vn[j][rows, hcols]))
            chunks.append(_cat(heads) + half_bias)
        mixed.append(jnp.concatenate(chunks, axis=0))
    g_b = slab_dots(4 * N_SLABS)
    sgu = _bf16(_cat([u[j] * mixed[j] for j in range(N_SLABS)]))
    y_b = [_dot(sgu, _col(wb_ref, j)) for j in range(N_SLABS)]

    merged = _bf16(_cat([_sigmoid_x2(g_a[j]) * y_a[j] + _sigmoid_x2(g_b[j]) * y_b[j]
                         for j in range(N_SLABS)]))
    for j in range(N_SLABS):
        o_ref[:, j * SLAB:(j + 1) * SLAB] = _col(x_ref, j) + _dot(merged, _col(wo_ref, j))
    _cast_rows(cast_in, cast_out, (None,) * n_cast)


def _ffn_kernel(seq_len, layer, final, n_cast, *refs):
    (x_ref, p_ref, g_ref, w_up_ref, cw_ref, cb_ref, w_down_ref, pg_ref, w_gate_ref,
     w_ple_ref, fg_ref) = refs[:_N_FFN_IN]
    cast_in = refs[_N_FFN_IN:_N_FFN_IN + n_cast]
    o_ref = refs[_N_FFN_IN + n_cast]
    cast_out = refs[_N_FFN_IN + n_cast + 1:_N_FFN_IN + 2 * n_cast + 1]
    up_ref, carry_ref, act_ref = refs[_N_FFN_IN + 2 * n_cast + 1:]
    g_ref, cb_ref, pg_ref = (_layer_row(r, layer) for r in (g_ref, cb_ref, pg_ref))
    tm = x_ref.shape[0]
    t0 = (pl.program_id(0) * tm) & (seq_len - 1)

    @pl.when(t0 == 0)
    def _():
        carry_ref[...] = jnp.zeros(carry_ref.shape, jnp.float32)

    x = x_ref[...]
    h = _rmsnorm_bf16(x, g_ref)

    def conv(slab):
        up = _dot(h, _col(w_up_ref, slab))
        halves = []
        for half in range(SLAB // LANES):
            s = 2 * slab + half
            cols = slice(s * LANES, (s + 1) * LANES)
            cur = up[:, half * LANES:(half + 1) * LANES]
            r = s % CONV_RING
            up_ref[r, 0:CONV_HALO, :] = carry_ref[s]
            up_ref[r, CONV_HALO:CONV_HALO + tm, :] = cur
            carry_ref[s] = cur[tm - CONV_HALO:, :]
            out = cb_ref[:, cols]
            for k in range(CONV_WIDTH - 1):
                lo = CONV_HALO - (CONV_WIDTH - 1) + k
                out = out + cw_ref[k:k + 1, cols] * up_ref[r, lo:lo + tm, :]
            halves.append(out + cw_ref[CONV_WIDTH - 1:CONV_WIDTH, cols] * cur)
        return _cat(halves)

    for j in range(FF_SLABS):
        a = conv(j)
        b = conv(FF_SLABS + j)
        act_ref[:, j * SLAB:(j + 1) * SLAB] = _bf16((0.5 * _gelu_x2(a)) * b)
    act = act_ref[...]
    x = [_col(x_ref, j) + _dot(act, _col(w_down_ref, j)) for j in range(N_SLABS)]

    x_scale = _rms_scale(x)
    hp = _bf16(_cat([x[j] * x_scale * _col(pg_ref, j) for j in range(N_SLABS)]))
    pb = _bf16(p_ref[...])
    for j in range(N_SLABS):
        gate = _sigmoid(_dot(hp, _col(w_gate_ref, j)))
        x[j] = x[j] + gate * _dot(pb, _col(w_ple_ref, j))
    if final:
        x_scale = _rms_scale(x)
        x = [x[j] * x_scale * _col(fg_ref, j) for j in range(N_SLABS)]
    for j in range(N_SLABS):
        o_ref[:, j * SLAB:(j + 1) * SLAB] = x[j]
    _cast_rows(cast_in, cast_out, MIXER_HALF_FROM[:n_cast])


def _layer_block(w, layer):
    rest = w.shape[1:]
    return pl.BlockSpec((None,) + rest, lambda i: (layer,) + (0,) * len(rest),
                        pipeline_mode=pl.Buffered(1))


def _whole(w):
    return pl.BlockSpec(w.shape, lambda i: (0,) * w.ndim, pipeline_mode=pl.Buffered(1))


def _padded_width(n):
    return n + (LANES if (n // LANES) % SUBLANES == 0 else 0)


def _cast_plan(w, layer, steps):
    _, k, n = w.shape
    rows = next(r for r in range(BF16_ROWS, k + 1, BF16_ROWS) if k % r == 0 and k // r <= steps)
    last = k // rows - 1
    npad = _padded_width(n)
    return (pl.BlockSpec((None, rows, n), lambda i: (layer, jnp.minimum(i, last), 0)),
            pl.BlockSpec((rows, npad), lambda i: (jnp.minimum(i, last), 0)),
            jax.ShapeDtypeStruct((k, npad), jnp.bfloat16))


def _compiler_params():
    return pltpu.CompilerParams(dimension_semantics=("arbitrary",),
                                vmem_limit_bytes=V7X_VMEM_LIMIT_BYTES)


def _call(body, name, tm, x, tile_inputs, tile_specs, params, param_specs, to_cast, cast_layer,
          scratch):
    n, d = x.shape
    steps = n // tm
    tile = pl.BlockSpec((tm, d), lambda i: (i, 0))
    plans = [_cast_plan(w, cast_layer, steps) for w in to_cast]
    out = pl.pallas_call(
        functools.partial(body, len(plans)),
        grid=(steps,),
        in_specs=[tile] + tile_specs + param_specs + [pl_in for pl_in, _, _ in plans],
        out_specs=[tile] + [pl_out for _, pl_out, _ in plans],
        out_shape=[jax.ShapeDtypeStruct((n, d), jnp.float32)] + [s for _, _, s in plans],
        scratch_shapes=scratch,
        compiler_params=_compiler_params(),
        name=name,
    )(x, *tile_inputs, *params, *to_cast)
    return out[0], out[1:]


def _cast_call(weights, layer, half_from):
    plans = [_cast_plan(w, layer, CAST_STEPS) for w in weights]
    return pl.pallas_call(
        functools.partial(_cast_kernel, half_from),
        grid=(CAST_STEPS,),
        in_specs=[p_in for p_in, _, _ in plans],
        out_specs=[p_out for _, p_out, _ in plans],
        out_shape=[s for _, _, s in plans],
        compiler_params=_compiler_params(),
        name="cast",
    )(*weights)


def kernel(x, p, mix_norm, w_in, w_pool, pool_scale, sgu_norm, w_spatial, b_spatial, w_branch_a, w_branch_b, w_out, ffn_norm, w_up, conv_w, conv_b, w_down, ple_norm, w_ple_gate, w_ple, final_norm):
    batch, seq_len, d = x.shape
    depth = w_in.shape[0]
    tm, tf = TOKEN_TILE, FFN_TOKEN_TILE
    assert d == D_MODEL and seq_len % tm == 0 and seq_len % tf == 0
    assert seq_len & (seq_len - 1) == 0 and tm % SGU_CHUNK == 0
    n = batch * seq_len
    bs = jnp.repeat(jnp.swapaxes(b_spatial, 1, 2), SGU_HEAD_DIM, axis=2)
    w_pool_rows = w_pool.reshape(depth, D_MODEL, POOL_GROUP_WIDTH)
    mixer_f32 = (w_in, w_pool_rows, w_branch_a, w_branch_b, w_out)
    ffn_f32 = (w_up, w_down, w_ple_gate, w_ple)
    mixer_scratch = [pltpu.VMEM((d // LANES, tm + POOL_HALO, LANES), jnp.float32),
                     pltpu.VMEM((2, d // LANES, tm + POOL_HALO, LANES), jnp.float32)]
    ffn_scratch = [pltpu.VMEM((CONV_RING, tf + CONV_HALO, LANES), jnp.float32),
                   pltpu.VMEM((2 * D_FF // LANES, CONV_HALO, LANES), jnp.float32),
                   pltpu.VMEM((tf, D_FF), jnp.bfloat16)]
    xf = x.reshape(n, d)
    pf = p.reshape(depth * n, PLE_DIM)
    fg = final_norm.reshape(1, d)
    mixer_bf16 = _cast_call(mixer_f32, 0, MIXER_HALF_FROM)
    for i in range(depth):
        last = i == depth - 1
        wi, wp, wa, wb, wo = mixer_bf16
        wp = wp.reshape(len(POOL_WINDOWS), POOL_GROUP_WIDTH, POOL_GROUP_WIDTH)
        params = (mix_norm, wi, wp, pool_scale, sgu_norm, w_spatial, bs, wa, wb, wo)
        specs = [_layer_block(q, i) if q.ndim > 2 and q.dtype == jnp.float32 else _whole(q)
                 for q in params]
        xf, (wu, wd, wg, wl) = _call(
            functools.partial(_mixer_kernel, seq_len, i), "mixer", tm, xf, (), [], params, specs,
            ffn_f32, i, mixer_scratch)
        params = (ffn_norm, wu, conv_w, conv_b, wd, ple_norm, wg, wl, fg)
        specs = [_layer_block(q, i) if q.ndim > 2 and q.dtype == jnp.float32 else _whole(q)
                 for q in params]
        p_tile = pl.BlockSpec((tf, PLE_DIM), lambda j, i=i: (i * (n // tf) + j, 0))
        xf, mixer_bf16 = _call(
            functools.partial(_ffn_kernel, seq_len, i, last), "ffn_final" if last else "ffn",
            tf, xf, (pf,), [p_tile], params, specs, () if last else mixer_f32, i + 1, ffn_scratch)
    return xf.reshape(batch, seq_len, d)
```
